```python
import math
import jax, jax.numpy as jnp
from jax import lax
import numpy as np

D_MODEL = 1024
BATCH = 16
SEQ = 4096
DEPTH = 4

SB_HEADS = 8
SB_HEAD_DIM = D_MODEL // SB_HEADS
SB_WIDTH = SB_HEADS * SB_HEAD_DIM
SB_BLOCK = 128
S5_WIDTH = D_MODEL // 2
S5_GROUP = 16
S5_GROUPS = S5_WIDTH // S5_GROUP
S5_STATE = 64
S5_DT_MIN = 1e-3
S5_DT_MAX = 1e-1
EVEN_IN = 4 * SB_WIDTH + 2 * S5_WIDTH
EVEN_MIX = SB_WIDTH + S5_WIDTH
GLA_HEADS = 4
GLA_KEY = D_MODEL // 2
GLA_VAL = D_MODEL
GLA_DK = GLA_KEY // GLA_HEADS
GLA_DV = GLA_VAL // GLA_HEADS
GLA_RANK = 16
GLA_TAU = 16.0
GLA_CHUNK = 64
ODD_IN = 2 * GLA_KEY + 2 * GLA_VAL + GLA_RANK
N_EVEN = (DEPTH + 1) // 2
N_ODD = DEPTH // 2
EPS = 1e-6

kernel_name = "hybrid_stickbreak_s5_gla_trunk"


def rms_norm(x, g):
    xf = x.astype(jnp.float32)
    y = xf * lax.rsqrt(jnp.mean(xf * xf, axis=-1, keepdims=True) + EPS)
    return (y * g.astype(jnp.float32)).astype(x.dtype)


def stick_breaking_attention(q, k, v):
    bsz, L, h, dh = q.shape
    qf = q.astype(jnp.float32) * (dh ** -0.5)
    kf = k.astype(jnp.float32)
    vf = v.astype(jnp.float32)
    outs = []
    for blk in range(L // SB_BLOCK):
        q0 = blk * SB_BLOCK
        kend = q0 + SB_BLOCK
        z = jnp.einsum('bthd,bshd->bhts', qf[:, q0:kend], kf[:, :kend])
        t_idx = q0 + jnp.arange(SB_BLOCK)[:, None]
        s_idx = jnp.arange(kend)[None, :]
        causal = s_idx < t_idx
        log_beta = jax.nn.log_sigmoid(z)
        log_one_minus = jnp.where(causal, log_beta - z, 0.0)
        rest = lax.cumsum(log_one_minus, axis=3, reverse=True) - log_one_minus
        w = jnp.where(causal, jnp.exp(log_beta + rest), 0.0)
        outs.append(jnp.einsum('bhts,bshd->bthd', w, vf[:, :kend]))
    return jnp.concatenate(outs, axis=1).astype(q.dtype)


def s5_mixer(u, lam_re, lam_im, log_dt, b_re, b_im, c_re, c_im, d_skip, w_glu, b_glu):
    f32 = jnp.float32
    bsz, L, _ = u.shape
    uf = u.astype(f32).reshape(bsz, L, S5_GROUPS, S5_GROUP)
    lam = lax.complex(lam_re.astype(f32), lam_im.astype(f32))
    dt = jnp.exp(log_dt.astype(f32))[:, None]
    lam_bar = jnp.exp(lam * dt)
    bmat = lax.complex(b_re.astype(f32), b_im.astype(f32))
    b_bar = ((lam_bar - 1.0) / lam)[..., None] * bmat
    bu = lax.complex(jnp.einsum('gnp,blgp->blgn', b_bar.real, uf),
                     jnp.einsum('gnp,blgp->blgn', b_bar.imag, uf))
    a = jnp.broadcast_to(lam_bar, bu.shape)

    def combine(left, right):
        a_l, x_l = left
        a_r, x_r = right
        return a_r * a_l, a_r * x_l + x_r

    _, hstate = lax.associative_scan(combine, (a, bu), axis=1)
    y = (jnp.einsum('gpn,blgn->blgp', c_re.astype(f32), hstate.real)
         - jnp.einsum('gpn,blgn->blgp', c_im.astype(f32), hstate.imag)
         + d_skip.astype(f32) * uf)
    y = jax.nn.gelu(y.reshape(bsz, L, S5_WIDTH))
    y = y * jax.nn.sigmoid(y @ w_glu.astype(f32) + b_glu.astype(f32))
    return y.astype(u.dtype)


def gla_chunked(q, k, v, log_a):
    f32 = jnp.float32
    bsz, L, h, dk = q.shape
    dv = v.shape[-1]
    c = GLA_CHUNK
    n = L // c
    qc = (q.astype(f32) * (dk ** -0.5)).reshape(bsz, n, c, h, dk)
    kc = k.astype(f32).reshape(bsz, n, c, h, dk)
    vc = v.astype(f32).reshape(bsz, n, c, h, dv)
    g = jnp.cumsum(log_a.astype(f32).reshape(bsz, n, c, h, dk), axis=2)
    g_last = g[:, :, -1]
    q_dec = qc * jnp.exp(g)
    k_inv = kc * jnp.exp(-g)
    k_dec = kc * jnp.exp(g_last[:, :, None] - g)
    scores = jnp.einsum('bnthk,bnshk->bnhts', q_dec, k_inv)
    mask = jnp.tril(jnp.ones((c, c), dtype=bool))
    scores = jnp.where(mask, scores, 0.0)
    o_intra = jnp.einsum('bnhts,bnshv->bnthv', scores, vc)

    def step(state, inp):
        qd, kd, vv, gl = inp
        o = jnp.einsum('bthk,bhkv->bthv', qd, state)
        state = jnp.exp(gl)[..., None] * state + jnp.einsum('bthk,bthv->bhkv', kd, vv)
        return state, o

    s0 = jnp.zeros((bsz, h, dk, dv), f32)
    _, o_inter = lax.scan(step, s0, (jnp.moveaxis(q_dec, 1, 0), jnp.moveaxis(k_dec, 1, 0),
                                     jnp.moveaxis(vc, 1, 0), jnp.moveaxis(g_last, 1, 0)))
    o = o_intra + jnp.moveaxis(o_inter, 0, 1)
    return o.reshape(bsz, L, h, dv)


def setup_inputs(seed: int = 0) -> dict:
    key = jax.random.key(seed)
    ks = jax.random.split(key, 24)
    f32 = jnp.float32
    nrm = lambda k, shape, s: jax.random.normal(k, shape, f32) * s
    x = jax.random.normal(ks[0], (BATCH, SEQ, D_MODEL), f32)
    even_norm_g = 1.0 + nrm(ks[1], (N_EVEN, D_MODEL), 0.02)
    even_w_in = nrm(ks[2], (N_EVEN, D_MODEL, EVEN_IN), D_MODEL ** -0.5)
    sb_q_norm_g = 1.0 + nrm(ks[3], (N_EVEN, SB_HEAD_DIM), 0.02)
    sb_k_norm_g = 1.0 + nrm(ks[4], (N_EVEN, SB_HEAD_DIM), 0.02)
    s5_lambda_re = -0.5 + nrm(ks[5], (N_EVEN, S5_GROUPS, S5_STATE), 0.01)
    n_idx = jnp.arange(S5_STATE, dtype=f32)
    s5_lambda_im = math.pi * n_idx + nrm(ks[6], (N_EVEN, S5_GROUPS, S5_STATE), 0.01)
    s5_log_dt = jax.random.uniform(ks[7], (N_EVEN, S5_GROUPS), f32,
                                   math.log(S5_DT_MIN), math.log(S5_DT_MAX))
    s5_b_re = nrm(ks[8], (N_EVEN, S5_GROUPS, S5_STATE, S5_GROUP), (2 * S5_GROUP) ** -0.5)
    s5_b_im = nrm(ks[9], (N_EVEN, S5_GROUPS, S5_STATE, S5_GROUP), (2 * S5_GROUP) ** -0.5)
    s5_c_re = nrm(ks[10], (N_EVEN, S5_GROUPS, S5_GROUP, S5_STATE), S5_STATE ** -0.5)
    s5_c_im = nrm(ks[11], (N_EVEN, S5_GROUPS, S5_GROUP, S5_STATE), S5_STATE ** -0.5)
    s5_d = nrm(ks[12], (N_EVEN, S5_GROUPS, S5_GROUP), 1.0)
    s5_w_glu = nrm(ks[13], (N_EVEN, S5_WIDTH, S5_WIDTH), S5_WIDTH ** -0.5)
    s5_b_glu = nrm(ks[14], (N_EVEN, S5_WIDTH), 0.02)
    even_w_out = nrm(ks[15], (N_EVEN, EVEN_MIX, D_MODEL), EVEN_MIX ** -0.5)
    odd_norm_g = 1.0 + nrm(ks[16], (N_ODD, D_MODEL), 0.02)
    odd_w_in = nrm(ks[17], (N_ODD, D_MODEL, ODD_IN), D_MODEL ** -0.5)
    gla_w_gate = nrm(ks[18], (N_ODD, GLA_RANK, GLA_KEY), GLA_RANK ** -0.5)
    gla_b_gate = nrm(ks[19], (N_ODD, GLA_KEY), 0.1)
    gla_o_norm_g = 1.0 + nrm(ks[20], (N_ODD, GLA_DV), 0.02)
    odd_w_out = nrm(ks[21], (N_ODD, GLA_VAL, D_MODEL), GLA_VAL ** -0.5)
    return {"x": x, "even_norm_g": even_norm_g, "even_w_in": even_w_in,
            "sb_q_norm_g": sb_q_norm_g, "sb_k_norm_g": sb_k_norm_g,
            "s5_lambda_re": s5_lambda_re, "s5_lambda_im": s5_lambda_im, "s5_log_dt": s5_log_dt,
            "s5_b_re": s5_b_re, "s5_b_im": s5_b_im, "s5_c_re": s5_c_re, "s5_c_im": s5_c_im,
            "s5_d": s5_d, "s5_w_glu": s5_w_glu, "s5_b_glu": s5_b_glu, "even_w_out": even_w_out,
            "odd_norm_g": odd_norm_g, "odd_w_in": odd_w_in, "gla_w_gate": gla_w_gate,
            "gla_b_gate": gla_b_gate, "gla_o_norm_g": gla_o_norm_g, "odd_w_out": odd_w_out}


def reference(x, even_norm_g, even_w_in, sb_q_norm_g, sb_k_norm_g, s5_lambda_re, s5_lambda_im,
              s5_log_dt, s5_b_re, s5_b_im, s5_c_re, s5_c_im, s5_d, s5_w_glu, s5_b_glu, even_w_out,
              odd_norm_g, odd_w_in, gla_w_gate, gla_b_gate, gla_o_norm_g, odd_w_out):
    bsz, L, _ = x.shape
    for layer in range(DEPTH):
        i = layer // 2
        if layer % 2 == 0:
            h = rms_norm(x, even_norm_g[i])
            proj = h @ even_w_in[i]
            q, k, v, z_a, u, z_b = jnp.split(
                proj, [SB_WIDTH, 2 * SB_WIDTH, 3 * SB_WIDTH, 4 * SB_WIDTH,
                       4 * SB_WIDTH + S5_WIDTH], axis=-1)
            q = rms_norm(q.reshape(bsz, L, SB_HEADS, SB_HEAD_DIM), sb_q_norm_g[i])
            k = rms_norm(k.reshape(bsz, L, SB_HEADS, SB_HEAD_DIM), sb_k_norm_g[i])
            v = v.reshape(bsz, L, SB_HEADS, SB_HEAD_DIM)
            o_a = stick_breaking_attention(q, k, v).reshape(bsz, L, SB_WIDTH) * jax.nn.silu(z_a)
            o_b = s5_mixer(u, s5_lambda_re[i], s5_lambda_im[i], s5_log_dt[i], s5_b_re[i],
                           s5_b_im[i], s5_c_re[i], s5_c_im[i], s5_d[i], s5_w_glu[i],
                           s5_b_glu[i]) * jax.nn.silu(z_b)
            x = x + jnp.concatenate([o_a, o_b], axis=-1) @ even_w_out[i]
        else:
            h = rms_norm(x, odd_norm_g[i])
            proj = h @ odd_w_in[i]
            q, k, v, z, r = jnp.split(
                proj, [GLA_KEY, 2 * GLA_KEY, 2 * GLA_KEY + GLA_VAL,
                       2 * GLA_KEY + 2 * GLA_VAL], axis=-1)
            log_a = jax.nn.log_sigmoid((r @ gla_w_gate[i] + gla_b_gate[i]).astype(jnp.float32)) / GLA_TAU
            o = gla_chunked(q.reshape(bsz, L, GLA_HEADS, GLA_DK),
                            k.reshape(bsz, L, GLA_HEADS, GLA_DK),
                            v.reshape(bsz, L, GLA_HEADS, GLA_DV),
                            log_a.reshape(bsz, L, GLA_HEADS, GLA_DK))
            o = rms_norm(o, gla_o_norm_g[i]).astype(x.dtype).reshape(bsz, L, GLA_VAL)
            x = x + (o * jax.nn.silu(z)) @ odd_w_out[i]
    return x
```

```python
import functools
import math

import jax
import jax.numpy as jnp
from jax import lax
from jax.experimental import pallas as pl
from jax.experimental.pallas import tpu as pltpu

F32 = jnp.float32
BF16 = jnp.bfloat16

EPS = 1e-6
SB_HEADS = 8
SB_HEAD_DIM = 128
S5_GROUP = 16
S5_STATE = 64
GLA_HEADS = 4
GLA_DK = 128
GLA_DV = 256
GLA_RANK = 16
GLA_TAU = 16.0
GLA_CHUNK = 64

LANES = 128
MXU_DIM = 256
VMEM_LIMIT = 56 * 1024 * 1024

ROW_TILE = 512
SB_BLOCK = 128
S5_TIME_TILE = 32
S5_SUB = 8
GLA_TIME_TILE = 512


def _dot(a, b):
    return jnp.dot(a, b, preferred_element_type=F32)


def _dot_nt(a, b):
    return lax.dot_general(a, b, (((1,), (1,)), ((), ())), preferred_element_type=F32)


def _rms(x, g):
    return x * lax.rsqrt(jnp.mean(x * x, axis=-1, keepdims=True) + EPS) * g


def _sigmoid(x):
    return 1.0 / (1.0 + jnp.exp(-x))


def _silu(x):
    return x * _sigmoid(x)


def _softplus(x):
    return jnp.maximum(x, 0.0) + jnp.log(1.0 + jnp.exp(-jnp.abs(x)))


def _split_bf16(x):
    hi = x.astype(BF16)
    lo = (x - hi.astype(F32)).astype(BF16)
    return hi, lo


def _params(n_axes=1):
    return pltpu.CompilerParams(dimension_semantics=("arbitrary",) * n_axes,
                                vmem_limit_bytes=VMEM_LIMIT)


def _full(shape):
    return pl.BlockSpec(shape, lambda *_: (0,) * len(shape))


def _even_in_kernel(x_ref, g_ref, w_ref, qg_ref, kg_ref,
                    q_ref, k_ref, v_ref, ga_ref, u_ref, gb_ref):
    width = SB_HEADS * SB_HEAD_DIM
    h = _rms(x_ref[...], g_ref[...]).astype(BF16)
    seg = 512

    def proj(lo):
        return _dot(h, w_ref[:, lo:lo + seg])

    for ref, gain_ref, off, scale in ((q_ref, qg_ref, 0, SB_HEAD_DIM ** -0.5),
                                      (k_ref, kg_ref, width, 1.0)):
        gain = gain_ref[...] * scale
        for s in range(width // seg):
            p = proj(off + s * seg)
            for hd in range(seg // SB_HEAD_DIM):
                ph = p[:, hd * SB_HEAD_DIM:(hd + 1) * SB_HEAD_DIM]
                lo = s * seg + hd * SB_HEAD_DIM
                ref[:, lo:lo + SB_HEAD_DIM] = _rms(ph, gain).astype(BF16)
    for s in range(width // seg):
        v_ref[:, s * seg:(s + 1) * seg] = proj(2 * width + s * seg).astype(BF16)
    for s in range(width // seg):
        ga_ref[:, s * seg:(s + 1) * seg] = _silu(proj(3 * width + s * seg)).astype(BF16)
    u_ref[...] = proj(4 * width)
    gb_ref[...] = _silu(proj(4 * width + seg))


def _even_in(x2, g, w, qg, kg):
    n, d = x2.shape
    width = SB_HEADS * SB_HEAD_DIM
    s5w = (w.shape[1] - 4 * width) // 2
    row = lambda c: pl.BlockSpec((ROW_TILE, c), lambda i: (i, 0))
    return pl.pallas_call(
        _even_in_kernel,
        grid=(n // ROW_TILE,),
        in_specs=[row(d), _full((1, d)), _full(w.shape), _full((1, SB_HEAD_DIM)),
                  _full((1, SB_HEAD_DIM))],
        out_specs=[row(width), row(width), row(width), row(width), row(s5w), row(s5w)],
        out_shape=[jax.ShapeDtypeStruct((n, width), BF16)] * 4
                  + [jax.ShapeDtypeStruct((n, s5w), F32)] * 2,
        compiler_params=_params(),
        name="even_in_proj",
    )(x2, g, w, qg, kg)


def _sb_attn_kernel(q_ref, k_ref, v_ref, gate_ref, uo_ref, o_ref):
    t = SB_BLOCK
    seq = q_ref.shape[1]
    uo = uo_ref[...]
    row = lax.broadcasted_iota(jnp.int32, (t, t), 0)
    col = lax.broadcasted_iota(jnp.int32, (t, t), 1)
    strict = col < row

    def block(q, kb, carry, acc, diag):
        kblk = k_ref[0, pl.ds(pl.multiple_of(kb * t, t), t), :]
        vblk = v_ref[0, pl.ds(pl.multiple_of(kb * t, t), t), :]
        z = _dot_nt(q, kblk)
        sp = _softplus(z)
        spm = jnp.where(strict, sp, 0.0) if diag else sp
        hi, lo = _split_bf16(spm)
        r2 = _dot(hi, uo) + _dot(lo, uo)
        rest = r2[:, :t] + carry
        w = jnp.exp(z - sp - rest)
        if diag:
            w = jnp.where(strict, w, 0.0)
        acc = acc + _dot(w.astype(BF16), vblk)
        carry = carry + r2[:, t:]
        return carry, acc

    def q_block(qi, _):
        rows = pl.ds(pl.multiple_of(qi * t, t), t)
        q = q_ref[0, rows, :]
        zeros = jnp.zeros((t, t), F32)
        carry, acc = block(q, qi, zeros, zeros, True)

        def k_step(j, ca):
            return block(q, qi - 1 - j, ca[0], ca[1], False)

        carry, acc = lax.fori_loop(0, qi, k_step, (carry, acc))
        o_ref[0, rows, :] = (acc * gate_ref[0, rows, :].astype(F32)).astype(BF16)
        return 0

    lax.fori_loop(0, seq // t, q_block, 0)


def _sb_attn(q, k, v, gate, uo):
    bsz, seq, width = q.shape
    spec = pl.BlockSpec((1, seq, SB_HEAD_DIM), lambda b, h: (b, 0, h))
    return pl.pallas_call(
        _sb_attn_kernel,
        grid=(bsz, width // SB_HEAD_DIM),
        in_specs=[spec, spec, spec, spec, _full(uo.shape)],
        out_specs=spec,
        out_shape=jax.ShapeDtypeStruct((bsz, seq, width), BF16),
        compiler_params=_params(2),
        name="sb_attention",
    )(q, k, v, gate, uo)


def _gelu_tanh(x):
    c = math.sqrt(2.0 / math.pi)
    return 0.5 * x * (1.0 + jnp.tanh(c * (x + 0.044715 * (x * x * x))))


def _s5_kernel(u_ref, gb_ref, bre_ref, bim_ref, cre_ref, cimn_ref, lre_ref, lim_ref,
               d_ref, wglu_ref, bglu_ref, o_ref,
               us_ref, ut_ref, bure_ref, buim_ref, hre_ref, him_ref,
               hsre_ref, hsim_ref, ys_ref, yn_ref):
    bsz, tt, width = u_ref.shape
    rows = bsz * tt
    nstate = bure_ref.shape[1]
    nsub = tt // S5_SUB
    nslab = width // LANES

    @pl.when(pl.program_id(0) == 0)
    def _():
        hsre_ref[...] = jnp.zeros_like(hsre_ref)
        hsim_ref[...] = jnp.zeros_like(hsim_ref)

    for c in range(nslab):
        for b in range(bsz):
            for sub in range(nsub):
                r0 = sub * bsz * S5_SUB + b * S5_SUB
                us_ref[c, r0:r0 + S5_SUB, :] = u_ref[b, sub * S5_SUB:(sub + 1) * S5_SUB,
                                                     c * LANES:(c + 1) * LANES]
    for sub in range(nsub):
        for t8 in range(S5_SUB):
            step = sub * S5_SUB + t8
            for c in range(nslab):
                ut_ref[step * bsz:(step + 1) * bsz, c * LANES:(c + 1) * LANES] = (
                    us_ref[c, pl.ds(sub * bsz * S5_SUB + t8, bsz, stride=S5_SUB), :])

    ut = ut_ref[...].astype(BF16)
    groups_per_tile = MXU_DIM // S5_STATE
    in_per_tile = groups_per_tile * S5_GROUP
    for j in range(nstate // MXU_DIM):
        kc = (j * in_per_tile) // MXU_DIM
        lhs = ut[:, kc * MXU_DIM:(kc + 1) * MXU_DIM]
        ksl = slice(kc * MXU_DIM, (kc + 1) * MXU_DIM)
        nsl = slice(j * MXU_DIM, (j + 1) * MXU_DIM)
        bure_ref[:, nsl] = _dot(lhs, bre_ref[ksl, nsl])
        buim_ref[:, nsl] = _dot(lhs, bim_ref[ksl, nsl])

    chunk = 512
    for c in range(nstate // chunk):
        sl = slice(c * chunk, (c + 1) * chunk)
        lr = jnp.broadcast_to(lre_ref[:, sl], (bsz, chunk))
        li = jnp.broadcast_to(lim_ref[:, sl], (bsz, chunk))
        hr = hsre_ref[:, sl]
        hi = hsim_ref[:, sl]
        for step in range(tt):
            rsl = slice(step * bsz, (step + 1) * bsz)
            hr, hi = (lr * hr - li * hi + bure_ref[rsl, sl],
                      lr * hi + li * hr + buim_ref[rsl, sl])
            hre_ref[rsl, sl] = hr.astype(BF16)
            him_ref[rsl, sl] = hi.astype(BF16)
        hsre_ref[:, sl] = hr
        hsim_ref[:, sl] = hi

    out_per_tile = MXU_DIM // in_per_tile
    for n in range(width // MXU_DIM):
        acc = None
        for k in range(n * out_per_tile, (n + 1) * out_per_tile):
            ksl = slice(k * MXU_DIM, (k + 1) * MXU_DIM)
            nsl = slice(n * MXU_DIM, (n + 1) * MXU_DIM)
            d = _dot(hre_ref[:, ksl], cre_ref[ksl, nsl]) + _dot(him_ref[:, ksl], cimn_ref[ksl, nsl])
            acc = d if acc is None else acc + d
        for half in range(MXU_DIM // LANES):
            ys_ref[n * (MXU_DIM // LANES) + half] = acc[:, half * LANES:(half + 1) * LANES]
    for b in range(bsz):
        for c in range(nslab):
            yn_ref[b * tt:(b + 1) * tt, c * LANES:(c + 1) * LANES] = (
                ys_ref[c, pl.ds(b, tt, stride=bsz), :])

    u_nat = u_ref[...].reshape(rows, width)
    y = _gelu_tanh(yn_ref[...] + d_ref[...] * u_nat)
    glu = _dot(y.astype(BF16), wglu_ref[...]) + bglu_ref[...]
    out = y * _sigmoid(glu) * gb_ref[...].reshape(rows, width)
    o_ref[...] = out.reshape(bsz, tt, width)


def _s5(u, gb, bre, bim, cre, cimn, lre, lim, dskip, wglu, bglu):
    bsz, seq, width = u.shape
    nstate = bre.shape[1]
    tt = S5_TIME_TILE
    rows = bsz * tt
    blk = pl.BlockSpec((bsz, tt, width), lambda i: (0, i, 0))
    return pl.pallas_call(
        _s5_kernel,
        grid=(seq // tt,),
        in_specs=[blk, blk, _full(bre.shape), _full(bim.shape), _full(cre.shape),
                  _full(cimn.shape), _full(lre.shape), _full(lim.shape), _full(dskip.shape),
                  _full(wglu.shape), _full(bglu.shape)],
        out_specs=blk,
        out_shape=jax.ShapeDtypeStruct((bsz, seq, width), F32),
        scratch_shapes=[
            pltpu.VMEM((width // LANES, rows, LANES), F32),
            pltpu.VMEM((rows, width), F32),
            pltpu.VMEM((rows, nstate), F32),
            pltpu.VMEM((rows, nstate), F32),
            pltpu.VMEM((rows, nstate), BF16),
            pltpu.VMEM((rows, nstate), BF16),
            pltpu.VMEM((bsz, nstate), F32),
            pltpu.VMEM((bsz, nstate), F32),
            pltpu.VMEM((width // LANES, rows, LANES), F32),
            pltpu.VMEM((rows, width), F32),
        ],
        compiler_params=_params(),
        name="s5_mixer",
    )(u, gb, bre, bim, cre, cimn, lre, lim, dskip, wglu, bglu)


def _s5_weights(lam_re, lam_im, log_dt, b_re, b_im, c_re, c_im):
    g, n = lam_re.shape
    p = b_re.shape[-1]
    lam = lax.complex(lam_re.astype(F32), lam_im.astype(F32))
    dt = jnp.exp(log_dt.astype(F32))[:, None]
    lam_bar = jnp.exp(lam * dt)
    b_bar = ((lam_bar - 1.0) / lam)[..., None] * lax.complex(b_re.astype(F32), b_im.astype(F32))
    eye = jnp.eye(g, dtype=F32)
    bd_in = lambda m: jnp.einsum('gnp,gh->gphn', m, eye).reshape(g * p, g * n)
    bd_out = lambda m: jnp.einsum('gpn,gh->gnhp', m, eye).reshape(g * n, g * p)
    return (bd_in(b_bar.real).astype(BF16), bd_in(b_bar.imag).astype(BF16),
            bd_out(c_re.astype(F32)).astype(BF16), bd_out(-c_im.astype(F32)).astype(BF16),
            lam_bar.real.reshape(1, g * n), lam_bar.imag.reshape(1, g * n))


def _out_kernel(*refs):
    x_ref, w_ref, o_ref = refs[0], refs[-2], refs[-1]
    acc = x_ref[...]
    off = 0
    for a_ref in refs[1:-2]:
        kdim = a_ref.shape[1]
        acc = acc + _dot(a_ref[...].astype(BF16), w_ref[off:off + kdim, :])
        off += kdim
    o_ref[...] = acc


def _out_proj(x2, acts, w):
    n, d = x2.shape
    row = lambda c: pl.BlockSpec((ROW_TILE, c), lambda i: (i, 0))
    return pl.pallas_call(
        _out_kernel,
        grid=(n // ROW_TILE,),
        in_specs=[row(d)] + [row(a.shape[1]) for a in acts] + [_full(w.shape)],
        out_specs=row(d),
        out_shape=jax.ShapeDtypeStruct((n, d), F32),
        compiler_params=_params(),
        name="out_proj",
    )(x2, *acts, w)


def _odd_in_kernel(x_ref, g_ref, w_ref, wr_ref, wg_ref, bg_ref,
                   q_ref, k_ref, v_ref, gz_ref, la_ref):
    key = GLA_HEADS * GLA_DK
    val = GLA_HEADS * GLA_DV
    h = _rms(x_ref[...], g_ref[...]).astype(BF16)
    seg = 512

    def proj(lo):
        return _dot(h, w_ref[:, lo:lo + seg])

    q_ref[...] = proj(0) * (GLA_DK ** -0.5)
    k_ref[...] = proj(key)
    for s in range(val // seg):
        v_ref[:, s * seg:(s + 1) * seg] = proj(2 * key + s * seg).astype(BF16)
    for s in range(val // seg):
        gz_ref[:, s * seg:(s + 1) * seg] = _silu(proj(2 * key + val + s * seg)).astype(BF16)
    r = _dot(h, wr_ref[...])
    pre = _dot(r.astype(BF16), wg_ref[...]) + bg_ref[...]
    la_ref[...] = -_softplus(-pre) * (1.0 / GLA_TAU)


def _odd_in(x2, g, w, wr, wg, bg):
    n, d = x2.shape
    key = GLA_HEADS * GLA_DK
    val = GLA_HEADS * GLA_DV
    row = lambda c: pl.BlockSpec((ROW_TILE, c), lambda i: (i, 0))
    return pl.pallas_call(
        _odd_in_kernel,
        grid=(n // ROW_TILE,),
        in_specs=[row(d), _full((1, d)), _full(w.shape), _full(wr.shape), _full(wg.shape),
                  _full(bg.shape)],
        out_specs=[row(key), row(key), row(val), row(val), row(key)],
        out_shape=[jax.ShapeDtypeStruct((n, key), F32), jax.ShapeDtypeStruct((n, key), F32),
                   jax.ShapeDtypeStruct((n, val), BF16), jax.ShapeDtypeStruct((n, val), BF16),
                   jax.ShapeDtypeStruct((n, key), F32)],
        compiler_params=_params(),
        name="odd_in_proj",
    )(x2, g, w, wr, wg, bg)


def _gla_kernel(q_ref, k_ref, la_ref, v_ref, gz_ref, tri_ref, og_ref, o_ref, s_ref):
    c = GLA_CHUNK
    tt = q_ref.shape[1]

    @pl.when(pl.program_id(1) == 0)
    def _():
        s_ref[...] = jnp.zeros_like(s_ref)

    tri = tri_ref[...]
    row = lax.broadcasted_iota(jnp.int32, (c, c), 0)
    col = lax.broadcasted_iota(jnp.int32, (c, c), 1)
    causal = col <= row
    og = og_ref[...]

    def chunk_step(i, _):
        rows = pl.ds(pl.multiple_of(i * c, c), c)
        for hd in range(GLA_HEADS):
            ksl = slice(hd * GLA_DK, (hd + 1) * GLA_DK)
            vsl = slice(hd * GLA_DV, (hd + 1) * GLA_DV)
            la = la_ref[0, rows, ksl]
            q = q_ref[0, rows, ksl]
            k = k_ref[0, rows, ksl]
            v = v_ref[0, rows, vsl]
            hi, lo = _split_bf16(la)
            g = _dot(tri, hi) + _dot(tri, lo)
            g_last = g[c - 1:c, :]
            q_dec = (q * jnp.exp(g)).astype(BF16)
            k_inv = (k * jnp.exp(-g)).astype(BF16)
            k_dec = k * jnp.exp(g_last - g)
            scores = jnp.where(causal, _dot_nt(q_dec, k_inv), 0.0)
            state = s_ref[hd]
            o = _dot(scores.astype(BF16), v) + _dot(q_dec, state.astype(BF16))
            decay_col = jnp.exp(jnp.sum(la.T, axis=1, keepdims=True))
            s_ref[hd] = decay_col * state + _dot(k_dec.T.astype(BF16), v)
            o = _rms(o, og) * gz_ref[0, rows, vsl].astype(F32)
            o_ref[0, rows, vsl] = o.astype(BF16)
        return 0

    lax.fori_loop(0, tt // c, chunk_step, 0)


def _gla(q, k, la, v, gz, tri, og):
    bsz, seq, key = q.shape
    val = v.shape[2]
    tt = GLA_TIME_TILE
    kspec = pl.BlockSpec((1, tt, key), lambda b, i: (b, i, 0))
    vspec = pl.BlockSpec((1, tt, val), lambda b, i: (b, i, 0))
    return pl.pallas_call(
        _gla_kernel,
        grid=(bsz, seq // tt),
        in_specs=[kspec, kspec, kspec, vspec, vspec, _full(tri.shape), _full(og.shape)],
        out_specs=vspec,
        out_shape=jax.ShapeDtypeStruct((bsz, seq, val), BF16),
        scratch_shapes=[pltpu.VMEM((GLA_HEADS, GLA_DK, GLA_DV), F32)],
        compiler_params=_params(2),
        name="gla",
    )(q, k, la, v, gz, tri, og)


def kernel(x, even_norm_g, even_w_in, sb_q_norm_g, sb_k_norm_g, s5_lambda_re, s5_lambda_im,
           s5_log_dt, s5_b_re, s5_b_im, s5_c_re, s5_c_im, s5_d, s5_w_glu, s5_b_glu, even_w_out,
           odd_norm_g, odd_w_in, gla_w_gate, gla_b_gate, gla_o_norm_g, odd_w_out):
    bsz, seq, d = x.shape
    n = bsz * seq
    depth = even_norm_g.shape[0] + odd_norm_g.shape[0]
    sb_width = SB_HEADS * SB_HEAD_DIM
    key = GLA_HEADS * GLA_DK
    val = GLA_HEADS * GLA_DV
    assert n % ROW_TILE == 0 and seq % SB_BLOCK == 0 and seq % GLA_TIME_TILE == 0
    assert seq % S5_TIME_TILE == 0 and bsz % 8 == 0

    t = SB_BLOCK
    j_idx = jnp.arange(t)[:, None]
    s_idx = jnp.arange(t)[None, :]
    uo = jnp.concatenate([(j_idx > s_idx), jnp.ones((t, t), bool)], axis=1).astype(BF16)
    c_idx = jnp.arange(GLA_CHUNK)
    tri = (c_idx[None, :] <= c_idx[:, None]).astype(BF16)

    x2 = x.reshape(n, d).astype(F32)
    for layer in range(depth):
        i = layer // 2
        if layer % 2 == 0:
            q, k, v, ga, u, gb = _even_in(
                x2, even_norm_g[i].reshape(1, d).astype(F32), even_w_in[i].astype(BF16),
                sb_q_norm_g[i].reshape(1, -1).astype(F32), sb_k_norm_g[i].reshape(1, -1).astype(F32))
            shp = (bsz, seq, sb_width)
            o_a = _sb_attn(q.reshape(shp), k.reshape(shp), v.reshape(shp), ga.reshape(shp), uo)
            s5w = u.shape[1]
            bre, bim, cre, cimn, lre, lim = _s5_weights(
                s5_lambda_re[i], s5_lambda_im[i], s5_log_dt[i], s5_b_re[i], s5_b_im[i],
                s5_c_re[i], s5_c_im[i])
            o_b = _s5(u.reshape(bsz, seq, s5w), gb.reshape(bsz, seq, s5w), bre, bim, cre, cimn,
                      lre, lim, s5_d[i].reshape(1, s5w).astype(F32), s5_w_glu[i].astype(BF16),
                      s5_b_glu[i].reshape(1, s5w).astype(F32))
            x2 = _out_proj(x2, [o_a.reshape(n, sb_width), o_b.reshape(n, s5w)],
                           even_w_out[i].astype(BF16))
        else:
            w = odd_w_in[i]
            main = 2 * key + 2 * val
            wr = jnp.pad(w[:, main:], ((0, 0), (0, LANES - GLA_RANK))).astype(BF16)
            wg = jnp.pad(gla_w_gate[i], ((0, LANES - GLA_RANK), (0, 0))).astype(BF16)
            q, k, v, gz, la = _odd_in(
                x2, odd_norm_g[i].reshape(1, d).astype(F32), w[:, :main].astype(BF16), wr, wg,
                gla_b_gate[i].reshape(1, key).astype(F32))
            o = _gla(q.reshape(bsz, seq, key), k.reshape(bsz, seq, key), la.reshape(bsz, seq, key),
                     v.reshape(bsz, seq, val), gz.reshape(bsz, seq, val), tri,
                     gla_o_norm_g[i].reshape(1, GLA_DV).astype(F32))
            x2 = _out_proj(x2, [o.reshape(n, val)], odd_w_out[i].astype(BF16))
    return x2.reshape(bsz, seq, d).astype(x.dtype)
```

```python
import functools
import math

import jax
import jax.numpy as jnp
from jax import lax
from jax.experimental import pallas as pl
from jax.experimental.pallas import tpu as pltpu

F32 = jnp.float32
BF16 = jnp.bfloat16

EPS = 1e-6
SB_HEADS = 8
SB_HEAD_DIM = 128
S5_GROUP = 16
S5_STATE = 64
GLA_HEADS = 4
GLA_DK = 128
GLA_DV = 256
GLA_RANK = 16
GLA_TAU = 16.0
GLA_CHUNK = 64

LANES = 128
MXU_DIM = 256
VMEM_LIMIT = 56 * 1024 * 1024

ROW_TILE = 512
SB_BLOCK = 128
SB_DEAD_LOG = 88.0
S5_TIME_TILE = 32
S5_SUB = 8
GLA_TIME_TILE = 512


def _dot(a, b):
    return jnp.dot(a, b, preferred_element_type=F32)


def _dot_nt(a, b):
    return lax.dot_general(a, b, (((1,), (1,)), ((), ())), preferred_element_type=F32)


def _rms(x, g):
    return x * lax.rsqrt(jnp.mean(x * x, axis=-1, keepdims=True) + EPS) * g


def _sigmoid(x):
    return 1.0 / (1.0 + jnp.exp(-x))


def _silu(x):
    return x * _sigmoid(x)


def _softplus(x):
    neg_abs = pltpu.bitcast(pltpu.bitcast(x, jnp.uint32) | jnp.uint32(0x80000000), F32)
    return jnp.maximum(x, 0.0) + jnp.log(1.0 + jnp.exp(neg_abs))


def _split_bf16(x):
    hi = x.astype(BF16)
    lo = (x - hi.astype(F32)).astype(BF16)
    return hi, lo


def _params(n_axes=1):
    return pltpu.CompilerParams(dimension_semantics=("arbitrary",) * n_axes,
                                vmem_limit_bytes=VMEM_LIMIT)


def _full(shape):
    return pl.BlockSpec(shape, lambda *_: (0,) * len(shape))


def _even_in_kernel(x_ref, g_ref, w_ref, qg_ref, kg_ref,
                    q_ref, k_ref, v_ref, ga_ref, u_ref, gb_ref):
    width = SB_HEADS * SB_HEAD_DIM
    h = _rms(x_ref[...], g_ref[...]).astype(BF16)
    seg = 512

    def proj(lo):
        return _dot(h, w_ref[:, lo:lo + seg])

    for ref, gain_ref, off, scale in ((q_ref, qg_ref, 0, SB_HEAD_DIM ** -0.5),
                                      (k_ref, kg_ref, width, 1.0)):
        gain = gain_ref[...] * scale
        for s in range(width // seg):
            p = proj(off + s * seg)
            for hd in range(seg // SB_HEAD_DIM):
                ph = p[:, hd * SB_HEAD_DIM:(hd + 1) * SB_HEAD_DIM]
                lo = s * seg + hd * SB_HEAD_DIM
                ref[:, lo:lo + SB_HEAD_DIM] = _rms(ph, gain).astype(BF16)
    for s in range(width // seg):
        v_ref[:, s * seg:(s + 1) * seg] = proj(2 * width + s * seg).astype(BF16)
    for s in range(width // seg):
        ga_ref[:, s * seg:(s + 1) * seg] = _silu(proj(3 * width + s * seg)).astype(BF16)
    u_ref[...] = proj(4 * width)
    gb_ref[...] = _silu(proj(4 * width + seg))


def _even_in(x2, g, w, qg, kg):
    n, d = x2.shape
    width = SB_HEADS * SB_HEAD_DIM
    s5w = (w.shape[1] - 4 * width) // 2
    row = lambda c: pl.BlockSpec((ROW_TILE, c), lambda i: (i, 0))
    return pl.pallas_call(
        _even_in_kernel,
        grid=(n // ROW_TILE,),
        in_specs=[row(d), _full((1, d)), _full(w.shape), _full((1, SB_HEAD_DIM)),
                  _full((1, SB_HEAD_DIM))],
        out_specs=[row(width), row(width), row(width), row(width), row(s5w), row(s5w)],
        out_shape=[jax.ShapeDtypeStruct((n, width), BF16)] * 4
                  + [jax.ShapeDtypeStruct((n, s5w), F32)] * 2,
        compiler_params=_params(),
        name="even_in_proj",
    )(x2, g, w, qg, kg)


def _sb_attn_kernel(q_ref, k_ref, v_ref, gate_ref, uo_ref, o_ref, carry_ref, acc_ref):
    t = SB_BLOCK
    heads = q_ref.shape[2] // SB_HEAD_DIM
    qi = pl.program_id(1)
    uo = uo_ref[...]
    row = lax.broadcasted_iota(jnp.int32, (t, t), 0)
    col = lax.broadcasted_iota(jnp.int32, (t, t), 1)
    strict = col < row

    lanes = [slice(hd * SB_HEAD_DIM, (hd + 1) * SB_HEAD_DIM) for hd in range(heads)]

    def sweep(blocks):
        rows = [pl.ds(pl.multiple_of(kb * t, t), t) for kb, _ in blocks]
        items = [(n, hd) for n in range(len(blocks)) for hd in range(heads)]
        zs = {(n, hd): _dot_nt(q_ref[0, :, lanes[hd]], k_ref[0, rows[n], lanes[hd]])
              for n, hd in items}
        sps = {it: _softplus(zs[it]) for it in items}
        r2s = {}
        for n, hd in items:
            sp = sps[n, hd]
            hi, lo = _split_bf16(jnp.where(strict, sp, 0.0) if blocks[n][1] else sp)
            r2s[n, hd] = _dot(jnp.concatenate([hi, lo], axis=1), uo)
        low = None
        for hd in range(heads):
            carry = None if blocks[0][1] else carry_ref[hd]
            pv = None
            for n, (_, diag) in enumerate(blocks):
                r2 = r2s[n, hd]
                rest = r2[:, :t] if carry is None else r2[:, :t] + carry
                carry = r2[:, t:] if carry is None else r2[:, t:] + carry
                w = jnp.exp(zs[n, hd] - sps[n, hd] - rest)
                if diag:
                    w = jnp.where(strict, w, 0.0)
                d = _dot(w.astype(BF16), v_ref[0, rows[n], lanes[hd]])
                pv = d if pv is None else pv + d
            carry_ref[hd] = carry
            low = carry if low is None else jnp.minimum(low, carry)
            acc_ref[hd] = pv if blocks[0][1] else acc_ref[hd] + pv
        return jnp.min(low) < SB_DEAD_LOG

    def cond(state):
        j, live = state
        return jnp.logical_and(j <= qi, live)

    def body(state):
        j, _ = state
        return j + 1, sweep([(qi - j, False)])

    live = lax.cond(qi == 0,
                    lambda: sweep([(qi, True)]),
                    lambda: sweep([(qi, True), (qi - 1, False)]))
    lax.while_loop(cond, body, (jnp.int32(2), live))
    for hd in range(heads):
        o_ref[0, :, lanes[hd]] = (acc_ref[hd] * gate_ref[0, :, lanes[hd]].astype(F32)).astype(BF16)


def _sb_attn(q, k, v, gate, uo):
    bsz, seq, width = q.shape
    t = SB_BLOCK
    heads = width // SB_HEAD_DIM
    qspec = pl.BlockSpec((1, t, width), lambda b, i: (b, i, 0))
    kspec = pl.BlockSpec((1, seq, width), lambda b, i: (b, 0, 0))
    return pl.pallas_call(
        _sb_attn_kernel,
        grid=(bsz, seq // t),
        in_specs=[qspec, kspec, kspec, qspec, _full(uo.shape)],
        out_specs=qspec,
        out_shape=jax.ShapeDtypeStruct((bsz, seq, width), BF16),
        scratch_shapes=[pltpu.VMEM((heads, t, t), F32),
                        pltpu.VMEM((heads, t, SB_HEAD_DIM), F32)],
        compiler_params=_params(2),
        name="sb_attention",
    )(q, k, v, gate, uo)


def _gelu_tanh(x):
    c = math.sqrt(2.0 / math.pi)
    return 0.5 * x * (1.0 + jnp.tanh(c * (x + 0.044715 * (x * x * x))))


def _s5_kernel(u_ref, gb_ref, bre_ref, bim_ref, cre_ref, cimn_ref, lre_ref, lim_ref,
               d_ref, wglu_ref, bglu_ref, o_ref,
               us_ref, ut_ref, bure_ref, buim_ref, hre_ref, him_ref,
               hsre_ref, hsim_ref, ys_ref, yn_ref):
    bsz, tt, width = u_ref.shape
    rows = bsz * tt
    nstate = bure_ref.shape[1]
    nsub = tt // S5_SUB
    nslab = width // LANES

    @pl.when(pl.program_id(0) == 0)
    def _():
        hsre_ref[...] = jnp.zeros_like(hsre_ref)
        hsim_ref[...] = jnp.zeros_like(hsim_ref)

    for c in range(nslab):
        for b in range(bsz):
            for sub in range(nsub):
                r0 = sub * bsz * S5_SUB + b * S5_SUB
                us_ref[c, r0:r0 + S5_SUB, :] = u_ref[b, sub * S5_SUB:(sub + 1) * S5_SUB,
                                                     c * LANES:(c + 1) * LANES]
    for sub in range(nsub):
        for t8 in range(S5_SUB):
            step = sub * S5_SUB + t8
            for c in range(nslab):
                ut_ref[step * bsz:(step + 1) * bsz, c * LANES:(c + 1) * LANES] = (
                    us_ref[c, pl.ds(sub * bsz * S5_SUB + t8, bsz, stride=S5_SUB), :])

    ut = ut_ref[...].astype(BF16)
    groups_per_tile = MXU_DIM // S5_STATE
    in_per_tile = groups_per_tile * S5_GROUP
    for j in range(nstate // MXU_DIM):
        kc = (j * in_per_tile) // MXU_DIM
        lhs = ut[:, kc * MXU_DIM:(kc + 1) * MXU_DIM]
        ksl = slice(kc * MXU_DIM, (kc + 1) * MXU_DIM)
        nsl = slice(j * MXU_DIM, (j + 1) * MXU_DIM)
        bure_ref[:, nsl] = _dot(lhs, bre_ref[ksl, nsl])
        buim_ref[:, nsl] = _dot(lhs, bim_ref[ksl, nsl])

    chunk = 512
    for c in range(nstate // chunk):
        sl = slice(c * chunk, (c + 1) * chunk)
        lr = jnp.broadcast_to(lre_ref[:, sl], (bsz, chunk))
        li = jnp.broadcast_to(lim_ref[:, sl], (bsz, chunk))
        hr = hsre_ref[:, sl]
        hi = hsim_ref[:, sl]
        for step in range(tt):
            rsl = slice(step * bsz, (step + 1) * bsz)
            hr, hi = (lr * hr - li * hi + bure_ref[rsl, sl],
                      lr * hi + li * hr + buim_ref[rsl, sl])
            hre_ref[rsl, sl] = hr.astype(BF16)
            him_ref[rsl, sl] = hi.astype(BF16)
        hsre_ref[:, sl] = hr
        hsim_ref[:, sl] = hi

    out_per_tile = MXU_DIM // in_per_tile
    for n in range(width // MXU_DIM):
        acc = None
        for k in range(n * out_per_tile, (n + 1) * out_per_tile):
            ksl = slice(k * MXU_DIM, (k + 1) * MXU_DIM)
            nsl = slice(n * MXU_DIM, (n + 1) * MXU_DIM)
            d = _dot(hre_ref[:, ksl], cre_ref[ksl, nsl]) + _dot(him_ref[:, ksl], cimn_ref[ksl, nsl])
            acc = d if acc is None else acc + d
        for half in range(MXU_DIM // LANES):
            ys_ref[n * (MXU_DIM // LANES) + half] = acc[:, half * LANES:(half + 1) * LANES]
    for b in range(bsz):
        for c in range(nslab):
            yn_ref[b * tt:(b + 1) * tt, c * LANES:(c + 1) * LANES] = (
                ys_ref[c, pl.ds(b, tt, stride=bsz), :])

    u_nat = u_ref[...].reshape(rows, width)
    y = _gelu_tanh(yn_ref[...] + d_ref[...] * u_nat)
    glu = _dot(y.astype(BF16), wglu_ref[...]) + bglu_ref[...]
    out = y * _sigmoid(glu) * gb_ref[...].reshape(rows, width)
    o_ref[...] = out.reshape(bsz, tt, width)


def _s5(u, gb, bre, bim, cre, cimn, lre, lim, dskip, wglu, bglu):
    bsz, seq, width = u.shape
    nstate = bre.shape[1]
    tt = S5_TIME_TILE
    rows = bsz * tt
    blk = pl.BlockSpec((bsz, tt, width), lambda i: (0, i, 0))
    return pl.pallas_call(
        _s5_kernel,
        grid=(seq // tt,),
        in_specs=[blk, blk, _full(bre.shape), _full(bim.shape), _full(cre.shape),
                  _full(cimn.shape), _full(lre.shape), _full(lim.shape), _full(dskip.shape),
                  _full(wglu.shape), _full(bglu.shape)],
        out_specs=blk,
        out_shape=jax.ShapeDtypeStruct((bsz, seq, width), F32),
        scratch_shapes=[
            pltpu.VMEM((width // LANES, rows, LANES), F32),
            pltpu.VMEM((rows, width), F32),
            pltpu.VMEM((rows, nstate), F32),
            pltpu.VMEM((rows, nstate), F32),
            pltpu.VMEM((rows, nstate), BF16),
            pltpu.VMEM((rows, nstate), BF16),
            pltpu.VMEM((bsz, nstate), F32),
            pltpu.VMEM((bsz, nstate), F32),
            pltpu.VMEM((width // LANES, rows, LANES), F32),
            pltpu.VMEM((rows, width), F32),
        ],
        compiler_params=_params(),
        name="s5_mixer",
    )(u, gb, bre, bim, cre, cimn, lre, lim, dskip, wglu, bglu)


def _s5_weights(lam_re, lam_im, log_dt, b_re, b_im, c_re, c_im):
    g, n = lam_re.shape
    p = b_re.shape[-1]
    lam = lax.complex(lam_re.astype(F32), lam_im.astype(F32))
    dt = jnp.exp(log_dt.astype(F32))[:, None]
    lam_bar = jnp.exp(lam * dt)
    b_bar = ((lam_bar - 1.0) / lam)[..., None] * lax.complex(b_re.astype(F32), b_im.astype(F32))
    eye = jnp.eye(g, dtype=F32)
    bd_in = lambda m: jnp.einsum('gnp,gh->gphn', m, eye).reshape(g * p, g * n)
    bd_out = lambda m: jnp.einsum('gpn,gh->gnhp', m, eye).reshape(g * n, g * p)
    return (bd_in(b_bar.real).astype(BF16), bd_in(b_bar.imag).astype(BF16),
            bd_out(c_re.astype(F32)).astype(BF16), bd_out(-c_im.astype(F32)).astype(BF16),
            lam_bar.real.reshape(1, g * n), lam_bar.imag.reshape(1, g * n))


def _out_kernel(*refs):
    x_ref, w_ref, o_ref = refs[0], refs[-2], refs[-1]
    acc = x_ref[...]
    off = 0
    for a_ref in refs[1:-2]:
        kdim = a_ref.shape[1]
        acc = acc + _dot(a_ref[...].astype(BF16), w_ref[off:off + kdim, :])
        off += kdim
    o_ref[...] = acc


def _out_proj(x2, acts, w):
    n, d = x2.shape
    row = lambda c: pl.BlockSpec((ROW_TILE, c), lambda i: (i, 0))
    return pl.pallas_call(
        _out_kernel,
        grid=(n // ROW_TILE,),
        in_specs=[row(d)] + [row(a.shape[1]) for a in acts] + [_full(w.shape)],
        out_specs=row(d),
        out_shape=jax.ShapeDtypeStruct((n, d), F32),
        compiler_params=_params(),
        name="out_proj",
    )(x2, *acts, w)


def _odd_in_kernel(x_ref, g_ref, w_ref, wr_ref, wg_ref, bg_ref,
                   q_ref, k_ref, v_ref, gz_ref, la_ref):
    key = GLA_HEADS * GLA_DK
    val = GLA_HEADS * GLA_DV
    h = _rms(x_ref[...], g_ref[...]).astype(BF16)
    seg = 512

    def proj(lo):
        return _dot(h, w_ref[:, lo:lo + seg])

    q_ref[...] = proj(0) * (GLA_DK ** -0.5)
    k_ref[...] = proj(key)
    for s in range(val // seg):
        v_ref[:, s * seg:(s + 1) * seg] = proj(2 * key + s * seg).astype(BF16)
    for s in range(val // seg):
        gz_ref[:, s * seg:(s + 1) * seg] = _silu(proj(2 * key + val + s * seg)).astype(BF16)
    r = _dot(h, wr_ref[...])
    pre = _dot(r.astype(BF16), wg_ref[...]) + bg_ref[...]
    la_ref[...] = -_softplus(-pre) * (1.0 / GLA_TAU)


def _odd_in(x2, g, w, wr, wg, bg):
    n, d = x2.shape
    key = GLA_HEADS * GLA_DK
    val = GLA_HEADS * GLA_DV
    row = lambda c: pl.BlockSpec((ROW_TILE, c), lambda i: (i, 0))
    return pl.pallas_call(
        _odd_in_kernel,
        grid=(n // ROW_TILE,),
        in_specs=[row(d), _full((1, d)), _full(w.shape), _full(wr.shape), _full(wg.shape),
                  _full(bg.shape)],
        out_specs=[row(key), row(key), row(val), row(val), row(key)],
        out_shape=[jax.ShapeDtypeStruct((n, key), F32), jax.ShapeDtypeStruct((n, key), F32),
                   jax.ShapeDtypeStruct((n, val), BF16), jax.ShapeDtypeStruct((n, val), BF16),
                   jax.ShapeDtypeStruct((n, key), F32)],
        compiler_params=_params(),
        name="odd_in_proj",
    )(x2, g, w, wr, wg, bg)


def _gla_kernel(q_ref, k_ref, la_ref, v_ref, gz_ref, tri_ref, og_ref, o_ref,
                st_ref, qd_ref, sc_ref, ut_ref):
    c = GLA_CHUNK
    nchunk = q_ref.shape[1] // c
    items = [(i, hd) for i in range(nchunk) for hd in range(GLA_HEADS)]

    @pl.when(pl.program_id(1) == 0)
    def _():
        st_ref[...] = jnp.zeros_like(st_ref)

    tri2 = tri_ref[...]
    row = lax.broadcasted_iota(jnp.int32, (c, c), 0)
    col = lax.broadcasted_iota(jnp.int32, (c, c), 1)
    causal = col <= row
    og = og_ref[...]
    rows = lambda i: slice(i * c, (i + 1) * c)
    ksl = lambda hd: slice(hd * GLA_DK, (hd + 1) * GLA_DK)
    vsl = lambda hd: slice(hd * GLA_DV, (hd + 1) * GLA_DV)

    gs = []
    for i, hd in items:
        hi, lo = _split_bf16(la_ref[0, rows(i), ksl(hd)])
        gs.append(_dot(tri2, jnp.concatenate([hi, lo], axis=0)))
    k_invs, k_decs, e_lasts = [], [], []
    for n, (i, hd) in enumerate(items):
        g = gs[n]
        g_last = g[c - 1:c, :]
        k = k_ref[0, rows(i), ksl(hd)]
        qd_ref[i, hd] = (q_ref[0, rows(i), ksl(hd)] * jnp.exp(g)).astype(BF16)
        k_invs.append((k * jnp.exp(-g)).astype(BF16))
        k_decs.append((k * jnp.exp(g_last - g)).astype(BF16))
        e_lasts.append(jnp.exp(g_last))
    for n, (i, hd) in enumerate(items):
        scores = jnp.where(causal, _dot_nt(qd_ref[i, hd], k_invs[n]), 0.0)
        sc_ref[i, hd] = scores.astype(BF16)
    for n, (i, hd) in enumerate(items):
        ut_ref[i, hd] = lax.dot_general(v_ref[0, rows(i), vsl(hd)], k_decs[n],
                                        (((0,), (0,)), ((), ())), preferred_element_type=F32)

    for n, (i, hd) in enumerate(items):
        st = st_ref[hd]
        o = _dot(sc_ref[i, hd], v_ref[0, rows(i), vsl(hd)]) + _dot_nt(qd_ref[i, hd], st.astype(BF16))
        st_ref[hd] = st * e_lasts[n] + ut_ref[i, hd]
        o = _rms(o, og) * gz_ref[0, rows(i), vsl(hd)].astype(F32)
        o_ref[0, rows(i), vsl(hd)] = o.astype(BF16)


def _gla(q, k, la, v, gz, tri2, og):
    bsz, seq, key = q.shape
    val = v.shape[2]
    tt = GLA_TIME_TILE
    nchunk = tt // GLA_CHUNK
    kspec = pl.BlockSpec((1, tt, key), lambda b, i: (b, i, 0))
    vspec = pl.BlockSpec((1, tt, val), lambda b, i: (b, i, 0))
    return pl.pallas_call(
        _gla_kernel,
        grid=(bsz, seq // tt),
        in_specs=[kspec, kspec, kspec, vspec, vspec, _full(tri2.shape), _full(og.shape)],
        out_specs=vspec,
        out_shape=jax.ShapeDtypeStruct((bsz, seq, val), BF16),
        scratch_shapes=[
            pltpu.VMEM((GLA_HEADS, GLA_DV, GLA_DK), F32),
            pltpu.VMEM((nchunk, GLA_HEADS, GLA_CHUNK, GLA_DK), BF16),
            pltpu.VMEM((nchunk, GLA_HEADS, GLA_CHUNK, GLA_CHUNK), BF16),
            pltpu.VMEM((nchunk, GLA_HEADS, GLA_DV, GLA_DK), F32),
        ],
        compiler_params=_params(2),
        name="gla",
    )(q, k, la, v, gz, tri2, og)


def kernel(x, even_norm_g, even_w_in, sb_q_norm_g, sb_k_norm_g, s5_lambda_re, s5_lambda_im,
           s5_log_dt, s5_b_re, s5_b_im, s5_c_re, s5_c_im, s5_d, s5_w_glu, s5_b_glu, even_w_out,
           odd_norm_g, odd_w_in, gla_w_gate, gla_b_gate, gla_o_norm_g, odd_w_out):
    bsz, seq, d = x.shape
    n = bsz * seq
    depth = even_norm_g.shape[0] + odd_norm_g.shape[0]
    sb_width = SB_HEADS * SB_HEAD_DIM
    key = GLA_HEADS * GLA_DK
    val = GLA_HEADS * GLA_DV
    assert n % ROW_TILE == 0 and seq % SB_BLOCK == 0 and seq % GLA_TIME_TILE == 0
    assert seq % S5_TIME_TILE == 0 and bsz % 8 == 0

    t = SB_BLOCK
    j_idx = jnp.arange(t)[:, None]
    s_idx = jnp.arange(t)[None, :]
    uo = jnp.concatenate([(j_idx > s_idx), jnp.ones((t, t), bool)], axis=1).astype(BF16)
    uo = jnp.concatenate([uo, uo], axis=0)
    c_idx = jnp.arange(GLA_CHUNK)
    tri = (c_idx[None, :] <= c_idx[:, None]).astype(BF16)
    tri = jnp.concatenate([tri, tri], axis=1)

    x2 = x.reshape(n, d).astype(F32)
    for layer in range(depth):
        i = layer // 2
        if layer % 2 == 0:
            q, k, v, ga, u, gb = _even_in(
                x2, even_norm_g[i].reshape(1, d).astype(F32), even_w_in[i].astype(BF16),
                sb_q_norm_g[i].reshape(1, -1).astype(F32), sb_k_norm_g[i].reshape(1, -1).astype(F32))
            shp = (bsz, seq, sb_width)
            o_a = _sb_attn(q.reshape(shp), k.reshape(shp), v.reshape(shp), ga.reshape(shp), uo)
            s5w = u.shape[1]
            bre, bim, cre, cimn, lre, lim = _s5_weights(
                s5_lambda_re[i], s5_lambda_im[i], s5_log_dt[i], s5_b_re[i], s5_b_im[i],
                s5_c_re[i], s5_c_im[i])
            o_b = _s5(u.reshape(bsz, seq, s5w), gb.reshape(bsz, seq, s5w), bre, bim, cre, cimn,
                      lre, lim, s5_d[i].reshape(1, s5w).astype(F32), s5_w_glu[i].astype(BF16),
                      s5_b_glu[i].reshape(1, s5w).astype(F32))
            x2 = _out_proj(x2, [o_a.reshape(n, sb_width), o_b.reshape(n, s5w)],
                           even_w_out[i].astype(BF16))
        else:
            w = odd_w_in[i]
            main = 2 * key + 2 * val
            wr = jnp.pad(w[:, main:], ((0, 0), (0, LANES - GLA_RANK))).astype(BF16)
            wg = jnp.pad(gla_w_gate[i], ((0, LANES - GLA_RANK), (0, 0))).astype(BF16)
            q, k, v, gz, la = _odd_in(
                x2, odd_norm_g[i].reshape(1, d).astype(F32), w[:, :main].astype(BF16), wr, wg,
                gla_b_gate[i].reshape(1, key).astype(F32))
            o = _gla(q.reshape(bsz, seq, key), k.reshape(bsz, seq, key), la.reshape(bsz, seq, key),
                     v.reshape(bsz, seq, val), gz.reshape(bsz, seq, val), tri,
                     gla_o_norm_g[i].reshape(1, GLA_DV).astype(F32))
            x2 = _out_proj(x2, [o.reshape(n, val)], odd_w_out[i].astype(BF16))
    return x2.reshape(bsz, seq, d).astype(x.dtype)
```

```python
import functools
import math

import jax
import jax.numpy as jnp
from jax import lax
from jax.experimental import pallas as pl
from jax.experimental.pallas import tpu as pltpu

F32 = jnp.float32
BF16 = jnp.bfloat16

EPS = 1e-6
SB_HEADS = 8
SB_HEAD_DIM = 128
S5_GROUP = 16
S5_STATE = 64
GLA_HEADS = 4
GLA_DK = 128
GLA_DV = 256
GLA_RANK = 16
GLA_TAU = 16.0
GLA_CHUNK = 64

LANES = 128
MXU_DIM = 256
VMEM_LIMIT = 56 * 1024 * 1024

ROW_TILE = 512
SB_BLOCK = 128
SB_DEAD_LOG = 88.0
S5_TIME_TILE = 32
S5_SUB = 8
GLA_TIME_TILE = 512


def _dot(a, b):
    return jnp.dot(a, b, preferred_element_type=F32)


def _dot_nt(a, b):
    return lax.dot_general(a, b, (((1,), (1,)), ((), ())), preferred_element_type=F32)


def _rms(x, g):
    return x * lax.rsqrt(jnp.mean(x * x, axis=-1, keepdims=True) + EPS) * g


def _sigmoid(x):
    return 1.0 / (1.0 + jnp.exp(-x))


def _silu(x):
    return x * _sigmoid(x)


def _softplus(x):
    neg_abs = pltpu.bitcast(pltpu.bitcast(x, jnp.uint32) | jnp.uint32(0x80000000), F32)
    return jnp.maximum(x, 0.0) + jnp.log(1.0 + jnp.exp(neg_abs))


def _split_bf16(x):
    hi = x.astype(BF16)
    lo = (x - hi.astype(F32)).astype(BF16)
    return hi, lo


def _params(n_axes=1):
    return pltpu.CompilerParams(dimension_semantics=("arbitrary",) * n_axes,
                                vmem_limit_bytes=VMEM_LIMIT)


def _full(shape):
    return pl.BlockSpec(shape, lambda *_: (0,) * len(shape))


def _even_in_body(x, g_ref, w_ref, qg_ref, kg_ref,
                  q_ref, k_ref, v_ref, ga_ref, u_ref, gb_ref):
    width = SB_HEADS * SB_HEAD_DIM
    h = _rms(x, g_ref[...]).astype(BF16)
    seg = 512

    def proj(lo):
        return _dot(h, w_ref[:, lo:lo + seg])

    for ref, gain_ref, off, scale in ((q_ref, qg_ref, 0, SB_HEAD_DIM ** -0.5),
                                      (k_ref, kg_ref, width, 1.0)):
        gain = gain_ref[...] * scale
        for s in range(width // seg):
            p = proj(off + s * seg)
            for hd in range(seg // SB_HEAD_DIM):
                ph = p[:, hd * SB_HEAD_DIM:(hd + 1) * SB_HEAD_DIM]
                lo = s * seg + hd * SB_HEAD_DIM
                ref[:, lo:lo + SB_HEAD_DIM] = _rms(ph, gain).astype(BF16)
    for s in range(width // seg):
        v_ref[:, s * seg:(s + 1) * seg] = proj(2 * width + s * seg).astype(BF16)
    for s in range(width // seg):
        ga_ref[:, s * seg:(s + 1) * seg] = _silu(proj(3 * width + s * seg)).astype(BF16)
    u_ref[...] = proj(4 * width)
    gb_ref[...] = _silu(proj(4 * width + seg))


def _io_kernel(*refs, n_acts, in_kind):
    x = refs[0][...]
    refs = refs[1:]
    n_in = {None: 0, "even": 4, "odd": 5}[in_kind]
    if n_acts:
        act_refs, wout_ref, refs = refs[:n_acts], refs[n_acts], refs[n_acts + 1:]
        off = 0
        for a_ref in act_refs:
            kdim = a_ref.shape[1]
            x = x + _dot(a_ref[...].astype(BF16), wout_ref[off:off + kdim, :])
            off += kdim
        refs[n_in][...] = x
        in_refs, out_refs = refs[:n_in], refs[n_in + 1:]
    else:
        in_refs, out_refs = refs[:n_in], refs[n_in:]
    if in_kind == "even":
        _even_in_body(x, *in_refs, *out_refs)
    elif in_kind == "odd":
        _odd_in_body(x, *in_refs, *out_refs)


def _layer_io(x2, acts, w_out, in_kind, in_params):
    n, d = x2.shape
    row = lambda c: pl.BlockSpec((ROW_TILE, c), lambda i: (i, 0))
    once = lambda a: pl.BlockSpec(a.shape, lambda i: (0,) * a.ndim, pipeline_mode=pl.Buffered(1))
    sds = jax.ShapeDtypeStruct
    args, in_specs, out_specs, out_shape = [x2], [row(d)], [], []
    if acts:
        args += list(acts) + [w_out]
        in_specs += [row(a.shape[1]) for a in acts] + [once(w_out)]
        out_specs.append(row(d))
        out_shape.append(sds((n, d), F32))
    args += list(in_params)
    in_specs += [once(p) for p in in_params]
    if in_kind == "even":
        width = SB_HEADS * SB_HEAD_DIM
        s5w = (in_params[1].shape[1] - 4 * width) // 2
        outs = [(width, BF16)] * 4 + [(s5w, F32)] * 2
    elif in_kind == "odd":
        key, val = GLA_HEADS * GLA_DK, GLA_HEADS * GLA_DV
        outs = [(key, F32), (key, F32), (val, BF16), (val, BF16), (key, F32)]
    else:
        outs = []
    out_specs += [row(c) for c, _ in outs]
    out_shape += [sds((n, c), dt) for c, dt in outs]
    res = pl.pallas_call(
        functools.partial(_io_kernel, n_acts=len(acts), in_kind=in_kind),
        grid=(n // ROW_TILE,),
        in_specs=in_specs,
        out_specs=out_specs,
        out_shape=out_shape,
        compiler_params=_params(),
        name=("out_" if acts else "") + (in_kind + "_in_" if in_kind else "") + "proj",
    )(*args)
    return (res[0], res[1:]) if acts else (x2, res)


def _sb_attn_kernel(q_ref, k_ref, v_ref, gate_ref, uo_ref, o_ref, carry_ref, acc_ref):
    t = SB_BLOCK
    heads = q_ref.shape[2] // SB_HEAD_DIM
    qi = pl.program_id(1)
    uo = uo_ref[...]
    row = lax.broadcasted_iota(jnp.int32, (t, t), 0)
    col = lax.broadcasted_iota(jnp.int32, (t, t), 1)
    strict = col < row

    lanes = [slice(hd * SB_HEAD_DIM, (hd + 1) * SB_HEAD_DIM) for hd in range(heads)]

    def sweep(blocks):
        rows = [pl.ds(pl.multiple_of(kb * t, t), t) for kb, _ in blocks]
        items = [(n, hd) for n in range(len(blocks)) for hd in range(heads)]
        zs = {(n, hd): _dot_nt(q_ref[0, :, lanes[hd]], k_ref[0, rows[n], lanes[hd]])
              for n, hd in items}
        sps = {it: _softplus(zs[it]) for it in items}
        r2s = {}
        for n, hd in items:
            sp = sps[n, hd]
            hi, lo = _split_bf16(jnp.where(strict, sp, 0.0) if blocks[n][1] else sp)
            r2s[n, hd] = _dot(jnp.concatenate([hi, lo], axis=1), uo)
        low = None
        for hd in range(heads):
            carry = None if blocks[0][1] else carry_ref[hd]
            pv = None
            for n, (_, diag) in enumerate(blocks):
                r2 = r2s[n, hd]
                rest = r2[:, :t] if carry is None else r2[:, :t] + carry
                carry = r2[:, t:] if carry is None else r2[:, t:] + carry
                w = jnp.exp(zs[n, hd] - sps[n, hd] - rest)
                if diag:
                    w = jnp.where(strict, w, 0.0)
                d = _dot(w.astype(BF16), v_ref[0, rows[n], lanes[hd]])
                pv = d if pv is None else pv + d
            carry_ref[hd] = carry
            low = carry if low is None else jnp.minimum(low, carry)
            acc_ref[hd] = pv if blocks[0][1] else acc_ref[hd] + pv
        return jnp.min(low) < SB_DEAD_LOG

    def cond(state):
        j, live = state
        return jnp.logical_and(j <= qi, live)

    def body(state):
        j, _ = state
        return j + 1, sweep([(qi - j, False)])

    live = lax.cond(qi == 0,
                    lambda: sweep([(qi, True)]),
                    lambda: sweep([(qi, True), (qi - 1, False)]))
    lax.while_loop(cond, body, (jnp.int32(2), live))
    for hd in range(heads):
        o_ref[0, :, lanes[hd]] = (acc_ref[hd] * gate_ref[0, :, lanes[hd]].astype(F32)).astype(BF16)


def _sb_attn(q, k, v, gate, uo):
    bsz, seq, width = q.shape
    t = SB_BLOCK
    heads = width // SB_HEAD_DIM
    qspec = pl.BlockSpec((1, t, width), lambda b, i: (b, i, 0))
    kspec = pl.BlockSpec((1, seq, width), lambda b, i: (b, 0, 0))
    return pl.pallas_call(
        _sb_attn_kernel,
        grid=(bsz, seq // t),
        in_specs=[qspec, kspec, kspec, qspec, _full(uo.shape)],
        out_specs=qspec,
        out_shape=jax.ShapeDtypeStruct((bsz, seq, width), BF16),
        scratch_shapes=[pltpu.VMEM((heads, t, t), F32),
                        pltpu.VMEM((heads, t, SB_HEAD_DIM), F32)],
        compiler_params=_params(2),
        name="sb_attention",
    )(q, k, v, gate, uo)


def _gelu_tanh(x):
    c = math.sqrt(2.0 / math.pi)
    return 0.5 * x * (1.0 + jnp.tanh(c * (x + 0.044715 * (x * x * x))))


def _s5_kernel(u_ref, gb_ref, bre_ref, bim_ref, cre_ref, cimn_ref, lre_ref, lim_ref,
               d_ref, wglu_ref, bglu_ref, o_ref,
               us_ref, ut_ref, bure_ref, buim_ref, hre_ref, him_ref,
               hsre_ref, hsim_ref, ys_ref, yn_ref):
    bsz, tt, width = u_ref.shape
    rows = bsz * tt
    nstate = bure_ref.shape[1]
    nsub = tt // S5_SUB
    nslab = width // LANES

    @pl.when(pl.program_id(0) == 0)
    def _():
        hsre_ref[...] = jnp.zeros_like(hsre_ref)
        hsim_ref[...] = jnp.zeros_like(hsim_ref)

    for c in range(nslab):
        for b in range(bsz):
            for sub in range(nsub):
                r0 = sub * bsz * S5_SUB + b * S5_SUB
                us_ref[c, r0:r0 + S5_SUB, :] = u_ref[b, sub * S5_SUB:(sub + 1) * S5_SUB,
                                                     c * LANES:(c + 1) * LANES]
    for sub in range(nsub):
        for t8 in range(S5_SUB):
            step = sub * S5_SUB + t8
            for c in range(nslab):
                ut_ref[step * bsz:(step + 1) * bsz, c * LANES:(c + 1) * LANES] = (
                    us_ref[c, pl.ds(sub * bsz * S5_SUB + t8, bsz, stride=S5_SUB), :])

    ut = ut_ref[...].astype(BF16)
    groups_per_tile = MXU_DIM // S5_STATE
    in_per_tile = groups_per_tile * S5_GROUP
    for j in range(nstate // MXU_DIM):
        kc = (j * in_per_tile) // MXU_DIM
        lhs = ut[:, kc * MXU_DIM:(kc + 1) * MXU_DIM]
        ksl = slice(kc * MXU_DIM, (kc + 1) * MXU_DIM)
        nsl = slice(j * MXU_DIM, (j + 1) * MXU_DIM)
        bure_ref[:, nsl] = _dot(lhs, bre_ref[ksl, nsl])
        buim_ref[:, nsl] = _dot(lhs, bim_ref[ksl, nsl])

    chunk = 512
    for c in range(nstate // chunk):
        sl = slice(c * chunk, (c + 1) * chunk)
        lr = jnp.broadcast_to(lre_ref[:, sl], (bsz, chunk))
        li = jnp.broadcast_to(lim_ref[:, sl], (bsz, chunk))
        hr = hsre_ref[:, sl]
        hi = hsim_ref[:, sl]
        for step in range(tt):
            rsl = slice(step * bsz, (step + 1) * bsz)
            hr, hi = (lr * hr - li * hi + bure_ref[rsl, sl],
                      lr * hi + li * hr + buim_ref[rsl, sl])
            hre_ref[rsl, sl] = hr.astype(BF16)
            him_ref[rsl, sl] = hi.astype(BF16)
        hsre_ref[:, sl] = hr
        hsim_ref[:, sl] = hi

    tiles_per_slab = (LANES // S5_GROUP) * S5_STATE // MXU_DIM
    for m in range(nslab):
        acc = None
        nsl = slice(m * LANES, (m + 1) * LANES)
        for k in range(m * tiles_per_slab, (m + 1) * tiles_per_slab):
            ksl = slice(k * MXU_DIM, (k + 1) * MXU_DIM)
            d = _dot(hre_ref[:, ksl], cre_ref[ksl, nsl]) + _dot(him_ref[:, ksl], cimn_ref[ksl, nsl])
            acc = d if acc is None else acc + d
        ys_ref[m] = acc
    for b in range(bsz):
        for c in range(nslab):
            yn_ref[b * tt:(b + 1) * tt, c * LANES:(c + 1) * LANES] = (
                ys_ref[c, pl.ds(b, tt, stride=bsz), :])

    u_nat = u_ref[...].reshape(rows, width)
    y = _gelu_tanh(yn_ref[...] + d_ref[...] * u_nat)
    glu = _dot(y.astype(BF16), wglu_ref[...]) + bglu_ref[...]
    out = y * _sigmoid(glu) * gb_ref[...].reshape(rows, width)
    o_ref[...] = out.reshape(bsz, tt, width)


def _s5(u, gb, bre, bim, cre, cimn, lre, lim, dskip, wglu, bglu):
    bsz, seq, width = u.shape
    nstate = bre.shape[1]
    tt = S5_TIME_TILE
    rows = bsz * tt
    blk = pl.BlockSpec((bsz, tt, width), lambda i: (0, i, 0))
    return pl.pallas_call(
        _s5_kernel,
        grid=(seq // tt,),
        in_specs=[blk, blk, _full(bre.shape), _full(bim.shape), _full(cre.shape),
                  _full(cimn.shape), _full(lre.shape), _full(lim.shape), _full(dskip.shape),
                  _full(wglu.shape), _full(bglu.shape)],
        out_specs=blk,
        out_shape=jax.ShapeDtypeStruct((bsz, seq, width), F32),
        scratch_shapes=[
            pltpu.VMEM((width // LANES, rows, LANES), F32),
            pltpu.VMEM((rows, width), F32),
            pltpu.VMEM((rows, nstate), F32),
            pltpu.VMEM((rows, nstate), F32),
            pltpu.VMEM((rows, nstate), BF16),
            pltpu.VMEM((rows, nstate), BF16),
            pltpu.VMEM((bsz, nstate), F32),
            pltpu.VMEM((bsz, nstate), F32),
            pltpu.VMEM((width // LANES, rows, LANES), F32),
            pltpu.VMEM((rows, width), F32),
        ],
        compiler_params=_params(),
        name="s5_mixer",
    )(u, gb, bre, bim, cre, cimn, lre, lim, dskip, wglu, bglu)


def _s5_weights(lam_re, lam_im, log_dt, b_re, b_im, c_re, c_im):
    g, n = lam_re.shape
    p = b_re.shape[-1]
    lam = lax.complex(lam_re.astype(F32), lam_im.astype(F32))
    dt = jnp.exp(log_dt.astype(F32))[:, None]
    lam_bar = jnp.exp(lam * dt)
    b_bar = ((lam_bar - 1.0) / lam)[..., None] * lax.complex(b_re.astype(F32), b_im.astype(F32))
    eye = jnp.eye(g, dtype=F32)
    bd_in = lambda m: jnp.einsum('gnp,gh->gphn', m, eye).reshape(g * p, g * n)
    bd_out = lambda m: jnp.einsum('gpn,gh->gnhp', m, eye).reshape(g * n, g * p)
    return (bd_in(b_bar.real).astype(BF16), bd_in(b_bar.imag).astype(BF16),
            bd_out(c_re.astype(F32)).astype(BF16), bd_out(-c_im.astype(F32)).astype(BF16),
            lam_bar.real.reshape(1, g * n), lam_bar.imag.reshape(1, g * n))


def _odd_in_body(x, g_ref, w_ref, wr_ref, wg_ref, bg_ref,
                 q_ref, k_ref, v_ref, gz_ref, la_ref):
    key = GLA_HEADS * GLA_DK
    val = GLA_HEADS * GLA_DV
    h = _rms(x, g_ref[...]).astype(BF16)
    seg = 512

    def proj(lo):
        return _dot(h, w_ref[:, lo:lo + seg])

    q_ref[...] = proj(0) * (GLA_DK ** -0.5)
    k_ref[...] = proj(key)
    for s in range(val // seg):
        v_ref[:, s * seg:(s + 1) * seg] = proj(2 * key + s * seg).astype(BF16)
    for s in range(val // seg):
        gz_ref[:, s * seg:(s + 1) * seg] = _silu(proj(2 * key + val + s * seg)).astype(BF16)
    r = _dot(h, wr_ref[...])
    pre = _dot(r.astype(BF16), wg_ref[...]) + bg_ref[...]
    la_ref[...] = -_softplus(-pre) * (1.0 / GLA_TAU)


def _gla_kernel(q_ref, k_ref, la_ref, v_ref, gz_ref, tri_ref, og_ref, o_ref,
                st_ref, qd_ref, sc_ref, ut_ref):
    c = GLA_CHUNK
    nchunk = q_ref.shape[1] // c
    items = [(i, hd) for i in range(nchunk) for hd in range(GLA_HEADS)]

    @pl.when(pl.program_id(1) == 0)
    def _():
        st_ref[...] = jnp.zeros_like(st_ref)

    tri2 = tri_ref[...]
    row = lax.broadcasted_iota(jnp.int32, (c, c), 0)
    col = lax.broadcasted_iota(jnp.int32, (c, c), 1)
    causal = col <= row
    og = og_ref[...]
    rows = lambda i: slice(i * c, (i + 1) * c)
    ksl = lambda hd: slice(hd * GLA_DK, (hd + 1) * GLA_DK)
    vsl = lambda hd: slice(hd * GLA_DV, (hd + 1) * GLA_DV)

    gs = []
    for i, hd in items:
        hi, lo = _split_bf16(la_ref[0, rows(i), ksl(hd)])
        gs.append(_dot(tri2, jnp.concatenate([hi, lo], axis=0)))
    k_invs, k_decs, e_lasts = [], [], []
    for n, (i, hd) in enumerate(items):
        g = gs[n]
        g_last = g[c - 1:c, :]
        k = k_ref[0, rows(i), ksl(hd)]
        qd_ref[i, hd] = (q_ref[0, rows(i), ksl(hd)] * jnp.exp(g)).astype(BF16)
        k_invs.append((k * jnp.exp(-g)).astype(BF16))
        k_decs.append((k * jnp.exp(g_last - g)).astype(BF16))
        e_lasts.append(jnp.exp(g_last))
    for n, (i, hd) in enumerate(items):
        scores = jnp.where(causal, _dot_nt(qd_ref[i, hd], k_invs[n]), 0.0)
        sc_ref[i, hd] = scores.astype(BF16)
    for n, (i, hd) in enumerate(items):
        ut_ref[i, hd] = lax.dot_general(v_ref[0, rows(i), vsl(hd)], k_decs[n],
                                        (((0,), (0,)), ((), ())), preferred_element_type=F32)

    for n, (i, hd) in enumerate(items):
        st = st_ref[hd]
        o = _dot(sc_ref[i, hd], v_ref[0, rows(i), vsl(hd)]) + _dot_nt(qd_ref[i, hd], st.astype(BF16))
        st_ref[hd] = st * e_lasts[n] + ut_ref[i, hd]
        o = _rms(o, og) * gz_ref[0, rows(i), vsl(hd)].astype(F32)
        o_ref[0, rows(i), vsl(hd)] = o.astype(BF16)


def _gla(q, k, la, v, gz, tri2, og):
    bsz, seq, key = q.shape
    val = v.shape[2]
    tt = GLA_TIME_TILE
    nchunk = tt // GLA_CHUNK
    kspec = pl.BlockSpec((1, tt, key), lambda b, i: (b, i, 0))
    vspec = pl.BlockSpec((1, tt, val), lambda b, i: (b, i, 0))
    return pl.pallas_call(
        _gla_kernel,
        grid=(bsz, seq // tt),
        in_specs=[kspec, kspec, kspec, vspec, vspec, _full(tri2.shape), _full(og.shape)],
        out_specs=vspec,
        out_shape=jax.ShapeDtypeStruct((bsz, seq, val), BF16),
        scratch_shapes=[
            pltpu.VMEM((GLA_HEADS, GLA_DV, GLA_DK), F32),
            pltpu.VMEM((nchunk, GLA_HEADS, GLA_CHUNK, GLA_DK), BF16),
            pltpu.VMEM((nchunk, GLA_HEADS, GLA_CHUNK, GLA_CHUNK), BF16),
            pltpu.VMEM((nchunk, GLA_HEADS, GLA_DV, GLA_DK), F32),
        ],
        compiler_params=_params(2),
        name="gla",
    )(q, k, la, v, gz, tri2, og)


def kernel(x, even_norm_g, even_w_in, sb_q_norm_g, sb_k_norm_g, s5_lambda_re, s5_lambda_im,
           s5_log_dt, s5_b_re, s5_b_im, s5_c_re, s5_c_im, s5_d, s5_w_glu, s5_b_glu, even_w_out,
           odd_norm_g, odd_w_in, gla_w_gate, gla_b_gate, gla_o_norm_g, odd_w_out):
    bsz, seq, d = x.shape
    n = bsz * seq
    depth = even_norm_g.shape[0] + odd_norm_g.shape[0]
    sb_width = SB_HEADS * SB_HEAD_DIM
    key = GLA_HEADS * GLA_DK
    val = GLA_HEADS * GLA_DV
    assert n % ROW_TILE == 0 and seq % SB_BLOCK == 0 and seq % GLA_TIME_TILE == 0
    assert seq % S5_TIME_TILE == 0 and bsz % 8 == 0

    t = SB_BLOCK
    j_idx = jnp.arange(t)[:, None]
    s_idx = jnp.arange(t)[None, :]
    uo = jnp.concatenate([(j_idx > s_idx), jnp.ones((t, t), bool)], axis=1).astype(BF16)
    uo = jnp.concatenate([uo, uo], axis=0)
    c_idx = jnp.arange(GLA_CHUNK)
    tri = (c_idx[None, :] <= c_idx[:, None]).astype(BF16)
    tri = jnp.concatenate([tri, tri], axis=1)

    def in_side(layer):
        i = layer // 2
        if layer >= depth:
            return None, []
        if layer % 2 == 0:
            return "even", [even_norm_g[i].reshape(1, d).astype(F32), even_w_in[i].astype(BF16),
                            sb_q_norm_g[i].reshape(1, -1).astype(F32),
                            sb_k_norm_g[i].reshape(1, -1).astype(F32)]
        w = odd_w_in[i]
        main = 2 * key + 2 * val
        wr = jnp.pad(w[:, main:], ((0, 0), (0, LANES - GLA_RANK))).astype(BF16)
        wg = jnp.pad(gla_w_gate[i], ((0, LANES - GLA_RANK), (0, 0))).astype(BF16)
        return "odd", [odd_norm_g[i].reshape(1, d).astype(F32), w[:, :main].astype(BF16), wr, wg,
                       gla_b_gate[i].reshape(1, key).astype(F32)]

    x2 = x.reshape(n, d).astype(F32)
    x2, proj = _layer_io(x2, [], None, *in_side(0))
    for layer in range(depth):
        i = layer // 2
        if layer % 2 == 0:
            q, k, v, ga, u, gb = proj
            shp = (bsz, seq, sb_width)
            o_a = _sb_attn(q.reshape(shp), k.reshape(shp), v.reshape(shp), ga.reshape(shp), uo)
            s5w = u.shape[1]
            bre, bim, cre, cimn, lre, lim = _s5_weights(
                s5_lambda_re[i], s5_lambda_im[i], s5_log_dt[i], s5_b_re[i], s5_b_im[i],
                s5_c_re[i], s5_c_im[i])
            o_b = _s5(u.reshape(bsz, seq, s5w), gb.reshape(bsz, seq, s5w), bre, bim, cre, cimn,
                      lre, lim, s5_d[i].reshape(1, s5w).astype(F32), s5_w_glu[i].astype(BF16),
                      s5_b_glu[i].reshape(1, s5w).astype(F32))
            acts, w_out = [o_a.reshape(n, sb_width), o_b.reshape(n, s5w)], even_w_out[i]
        else:
            q, k, v, gz, la = proj
            o = _gla(q.reshape(bsz, seq, key), k.reshape(bsz, seq, key), la.reshape(bsz, seq, key),
                     v.reshape(bsz, seq, val), gz.reshape(bsz, seq, val), tri,
                     gla_o_norm_g[i].reshape(1, GLA_DV).astype(F32))
            acts, w_out = [o.reshape(n, val)], odd_w_out[i]
        x2, proj = _layer_io(x2, acts, w_out.astype(BF16), *in_side(layer + 1))
    return x2.reshape(bsz, seq, d).astype(x.dtype)
```

```python
import functools
import math

import jax
import jax.numpy as jnp
from jax import lax
from jax.experimental import pallas as pl
from jax.experimental.pallas import tpu as pltpu

F32 = jnp.float32
BF16 = jnp.bfloat16

EPS = 1e-6
SB_HEADS = 8
SB_HEAD_DIM = 128
S5_GROUP = 16
S5_STATE = 64
GLA_HEADS = 4
GLA_DK = 128
GLA_DV = 256
GLA_RANK = 16
GLA_TAU = 16.0
GLA_CHUNK = 64

LANES = 128
MXU_DIM = 256
VMEM_LIMIT = 56 * 1024 * 1024

ROW_TILE = 512
SB_BLOCK = 128
SB_QGROUPS = 2
SB_DEAD_LOG = 88.0
S5_TIME_TILE = 32
S5_SUB = 8
GLA_TIME_TILE = 1024


def _dot(a, b):
    return jnp.dot(a, b, preferred_element_type=F32)


def _dot_nt(a, b):
    return lax.dot_general(a, b, (((1,), (1,)), ((), ())), preferred_element_type=F32)


def _rms(x, g):
    return x * lax.rsqrt(jnp.mean(x * x, axis=-1, keepdims=True) + EPS) * g


def _sigmoid(x):
    return 1.0 / (1.0 + jnp.exp(-x))


def _silu(x):
    return x * _sigmoid(x)


def _softplus(x):
    neg_abs = pltpu.bitcast(pltpu.bitcast(x, jnp.uint32) | jnp.uint32(0x80000000), F32)
    return jnp.maximum(x, 0.0) + jnp.log(1.0 + jnp.exp(neg_abs))


def _split_bf16(x):
    hi = x.astype(BF16)
    lo = (x - hi.astype(F32)).astype(BF16)
    return hi, lo


def _params(n_axes=1):
    return pltpu.CompilerParams(dimension_semantics=("arbitrary",) * n_axes,
                                vmem_limit_bytes=VMEM_LIMIT)


def _full(shape):
    return pl.BlockSpec(shape, lambda *_: (0,) * len(shape))


def _even_in_body(x, g_ref, w_ref, qg_ref, kg_ref,
                  q_ref, k_ref, v_ref, ga_ref, u_ref, gb_ref):
    width = SB_HEADS * SB_HEAD_DIM
    h = _rms(x, g_ref[...]).astype(BF16)
    seg = 512

    def proj(lo):
        return _dot(h, w_ref[:, lo:lo + seg])

    for ref, gain_ref, off, scale in ((q_ref, qg_ref, 0, SB_HEAD_DIM ** -0.5),
                                      (k_ref, kg_ref, width, 1.0)):
        gain = gain_ref[...] * scale
        for s in range(width // seg):
            p = proj(off + s * seg)
            for hd in range(seg // SB_HEAD_DIM):
                ph = p[:, hd * SB_HEAD_DIM:(hd + 1) * SB_HEAD_DIM]
                lo = s * seg + hd * SB_HEAD_DIM
                ref[:, lo:lo + SB_HEAD_DIM] = _rms(ph, gain).astype(BF16)
    for s in range(width // seg):
        v_ref[:, s * seg:(s + 1) * seg] = proj(2 * width + s * seg).astype(BF16)
    for s in range(width // seg):
        ga_ref[:, s * seg:(s + 1) * seg] = _silu(proj(3 * width + s * seg)).astype(BF16)
    u_ref[...] = proj(4 * width)
    gb_ref[...] = _silu(proj(4 * width + seg))


def _io_kernel(*refs, n_acts, in_kind):
    x = refs[0][...]
    refs = refs[1:]
    n_in = {None: 0, "even": 4, "odd": 5}[in_kind]
    if n_acts:
        act_refs, wout_ref, refs = refs[:n_acts], refs[n_acts], refs[n_acts + 1:]
        off = 0
        for a_ref in act_refs:
            kdim = a_ref.shape[1]
            x = x + _dot(a_ref[...].astype(BF16), wout_ref[off:off + kdim, :])
            off += kdim
        refs[n_in][...] = x
        in_refs, out_refs = refs[:n_in], refs[n_in + 1:]
    else:
        in_refs, out_refs = refs[:n_in], refs[n_in:]
    if in_kind == "even":
        _even_in_body(x, *in_refs, *out_refs)
    elif in_kind == "odd":
        _odd_in_body(x, *in_refs, *out_refs)


def _layer_io(x2, acts, w_out, in_kind, in_params):
    n, d = x2.shape
    row = lambda c: pl.BlockSpec((ROW_TILE, c), lambda i: (i, 0))
    once = lambda a: pl.BlockSpec(a.shape, lambda i: (0,) * a.ndim, pipeline_mode=pl.Buffered(1))
    sds = jax.ShapeDtypeStruct
    args, in_specs, out_specs, out_shape = [x2], [row(d)], [], []
    if acts:
        args += list(acts) + [w_out]
        in_specs += [row(a.shape[1]) for a in acts] + [once(w_out)]
        out_specs.append(row(d))
        out_shape.append(sds((n, d), F32))
    args += list(in_params)
    in_specs += [once(p) for p in in_params]
    if in_kind == "even":
        width = SB_HEADS * SB_HEAD_DIM
        s5w = (in_params[1].shape[1] - 4 * width) // 2
        outs = [(width, BF16)] * 4 + [(s5w, F32)] * 2
    elif in_kind == "odd":
        key, val = GLA_HEADS * GLA_DK, GLA_HEADS * GLA_DV
        outs = [(key, F32), (key, F32), (val, BF16), (val, BF16), (key, F32)]
    else:
        outs = []
    out_specs += [row(c) for c, _ in outs]
    out_shape += [sds((n, c), dt) for c, dt in outs]
    res = pl.pallas_call(
        functools.partial(_io_kernel, n_acts=len(acts), in_kind=in_kind),
        grid=(n // ROW_TILE,),
        in_specs=in_specs,
        out_specs=out_specs,
        out_shape=out_shape,
        compiler_params=_params(),
        name=("out_" if acts else "") + (in_kind + "_in_" if in_kind else "") + "proj",
    )(*args)
    return (res[0], res[1:]) if acts else (x2, res)


def _sb_attn_kernel(q_ref, k_ref, v_ref, gate_ref, uo_ref, o_ref, carry_ref, acc_ref):
    t = SB_BLOCK
    ng = SB_QGROUPS
    heads = q_ref.shape[2] // SB_HEAD_DIM
    base = pl.program_id(1) * ng
    uo = uo_ref[...]
    row = lax.broadcasted_iota(jnp.int32, (t, t), 0)
    col = lax.broadcasted_iota(jnp.int32, (t, t), 1)
    strict = col < row
    lanes = [slice(hd * SB_HEAD_DIM, (hd + 1) * SB_HEAD_DIM) for hd in range(heads)]
    qrows = [slice(g * t, (g + 1) * t) for g in range(ng)]

    def sweep(blocks):
        krows = {(g, n): pl.ds(pl.multiple_of(kb * t, t), t)
                 for g, blk in blocks.items() for n, (kb, _) in enumerate(blk)}
        items = [(g, n, hd) for (g, n) in krows for hd in range(heads)]
        zs = {(g, n, hd): _dot_nt(q_ref[0, qrows[g], lanes[hd]], k_ref[0, krows[g, n], lanes[hd]])
              for g, n, hd in items}
        sps = {it: _softplus(zs[it]) for it in items}
        r2s = {}
        for g, n, hd in items:
            sp = sps[g, n, hd]
            hi, lo = _split_bf16(jnp.where(strict, sp, 0.0) if blocks[g][n][1] else sp)
            r2s[g, n, hd] = _dot(jnp.concatenate([hi, lo], axis=1), uo)
        low = None
        for g, blk in blocks.items():
            fresh = blk[0][1]
            for hd in range(heads):
                carry = None if fresh else carry_ref[g, hd]
                pv = None
                for n, (_, diag) in enumerate(blk):
                    r2 = r2s[g, n, hd]
                    rest = r2[:, :t] if carry is None else r2[:, :t] + carry
                    carry = r2[:, t:] if carry is None else r2[:, t:] + carry
                    w = jnp.exp(zs[g, n, hd] - sps[g, n, hd] - rest)
                    if diag:
                        w = jnp.where(strict, w, 0.0)
                    d = _dot(w.astype(BF16), v_ref[0, krows[g, n], lanes[hd]])
                    pv = d if pv is None else pv + d
                carry_ref[g, hd] = carry
                low = carry if low is None else jnp.minimum(low, carry)
                acc_ref[g, hd] = pv if fresh else acc_ref[g, hd] + pv
        return jnp.min(low) < SB_DEAD_LOG

    def first_step():
        sweep({g: [(g - n, n == 0) for n in range(g + 1)] for g in range(ng)})
        return jnp.bool_(False)

    def later_step():
        return sweep({g: [(base + g, True), (base + g - 1, False)] for g in range(ng)})

    def cond(state):
        j, live = state
        return jnp.logical_and(j <= base, live)

    def body(state):
        j, _ = state
        return j + 1, sweep({g: [(base + g - j, False)] for g in range(ng)})

    _, live = lax.while_loop(cond, body, (jnp.int32(2), lax.cond(base == 0, first_step, later_step)))

    for extra in range(1, ng):
        @pl.when(live)
        def _():
            sweep({g: [(g - extra, False)] for g in range(extra, ng)})

    for g in range(ng):
        for hd in range(heads):
            o_ref[0, qrows[g], lanes[hd]] = (
                acc_ref[g, hd] * gate_ref[0, qrows[g], lanes[hd]].astype(F32)).astype(BF16)


def _sb_attn(q, k, v, gate, uo):
    bsz, seq, width = q.shape
    t = SB_BLOCK
    tq = SB_QGROUPS * t
    heads = width // SB_HEAD_DIM
    qspec = pl.BlockSpec((1, tq, width), lambda b, i: (b, i, 0))
    kspec = pl.BlockSpec((1, seq, width), lambda b, i: (b, 0, 0))
    return pl.pallas_call(
        _sb_attn_kernel,
        grid=(bsz, seq // tq),
        in_specs=[qspec, kspec, kspec, qspec, _full(uo.shape)],
        out_specs=qspec,
        out_shape=jax.ShapeDtypeStruct((bsz, seq, width), BF16),
        scratch_shapes=[pltpu.VMEM((SB_QGROUPS, heads, t, t), F32),
                        pltpu.VMEM((SB_QGROUPS, heads, t, SB_HEAD_DIM), F32)],
        compiler_params=_params(2),
        name="sb_attention",
    )(q, k, v, gate, uo)


def _gelu_tanh(x):
    c = math.sqrt(2.0 / math.pi)
    return 0.5 * x * (1.0 + jnp.tanh(c * (x + 0.044715 * (x * x * x))))


def _s5_kernel(u_ref, gb_ref, bre_ref, bim_ref, cre_ref, cimn_ref, lre_ref, lim_ref,
               d_ref, wglu_ref, bglu_ref, o_ref,
               us_ref, ut_ref, bure_ref, buim_ref, hre_ref, him_ref,
               hsre_ref, hsim_ref, ys_ref, yn_ref):
    bsz, tt, width = u_ref.shape
    rows = bsz * tt
    nstate = bure_ref.shape[1]
    nsub = tt // S5_SUB
    nslab = width // LANES

    @pl.when(pl.program_id(0) == 0)
    def _():
        hsre_ref[...] = jnp.zeros_like(hsre_ref)
        hsim_ref[...] = jnp.zeros_like(hsim_ref)

    for c in range(nslab):
        for b in range(bsz):
            for sub in range(nsub):
                r0 = sub * bsz * S5_SUB + b * S5_SUB
                us_ref[c, r0:r0 + S5_SUB, :] = u_ref[b, sub * S5_SUB:(sub + 1) * S5_SUB,
                                                     c * LANES:(c + 1) * LANES]
    for sub in range(nsub):
        for t8 in range(S5_SUB):
            step = sub * S5_SUB + t8
            for c in range(nslab):
                ut_ref[step * bsz:(step + 1) * bsz, c * LANES:(c + 1) * LANES] = (
                    us_ref[c, pl.ds(sub * bsz * S5_SUB + t8, bsz, stride=S5_SUB), :])

    ut = ut_ref[...].astype(BF16)
    groups_per_tile = MXU_DIM // S5_STATE
    in_per_tile = groups_per_tile * S5_GROUP
    for j in range(nstate // MXU_DIM):
        kc = (j * in_per_tile) // MXU_DIM
        lhs = ut[:, kc * MXU_DIM:(kc + 1) * MXU_DIM]
        ksl = slice(kc * MXU_DIM, (kc + 1) * MXU_DIM)
        nsl = slice(j * MXU_DIM, (j + 1) * MXU_DIM)
        bure_ref[:, nsl] = _dot(lhs, bre_ref[ksl, nsl])
        buim_ref[:, nsl] = _dot(lhs, bim_ref[ksl, nsl])

    chunk = 512
    for c in range(nstate // chunk):
        sl = slice(c * chunk, (c + 1) * chunk)
        lr = jnp.broadcast_to(lre_ref[:, sl], (bsz, chunk))
        li = jnp.broadcast_to(lim_ref[:, sl], (bsz, chunk))
        hr = hsre_ref[:, sl]
        hi = hsim_ref[:, sl]
        for step in range(tt):
            rsl = slice(step * bsz, (step + 1) * bsz)
            hr, hi = (lr * hr - li * hi + bure_ref[rsl, sl],
                      lr * hi + li * hr + buim_ref[rsl, sl])
            hre_ref[rsl, sl] = hr.astype(BF16)
            him_ref[rsl, sl] = hi.astype(BF16)
        hsre_ref[:, sl] = hr
        hsim_ref[:, sl] = hi

    tiles_per_slab = (LANES // S5_GROUP) * S5_STATE // MXU_DIM
    for m in range(nslab):
        acc = None
        nsl = slice(m * LANES, (m + 1) * LANES)
        for k in range(m * tiles_per_slab, (m + 1) * tiles_per_slab):
            ksl = slice(k * MXU_DIM, (k + 1) * MXU_DIM)
            d = _dot(hre_ref[:, ksl], cre_ref[ksl, nsl]) + _dot(him_ref[:, ksl], cimn_ref[ksl, nsl])
            acc = d if acc is None else acc + d
        ys_ref[m] = acc
    for b in range(bsz):
        for c in range(nslab):
            yn_ref[b * tt:(b + 1) * tt, c * LANES:(c + 1) * LANES] = (
                ys_ref[c, pl.ds(b, tt, stride=bsz), :])

    u_nat = u_ref[...].reshape(rows, width)
    y = _gelu_tanh(yn_ref[...] + d_ref[...] * u_nat)
    glu = _dot(y.astype(BF16), wglu_ref[...]) + bglu_ref[...]
    out = y * _sigmoid(glu) * gb_ref[...].reshape(rows, width)
    o_ref[...] = out.reshape(bsz, tt, width)


def _s5(u, gb, bre, bim, cre, cimn, lre, lim, dskip, wglu, bglu):
    bsz, seq, width = u.shape
    nstate = bre.shape[1]
    tt = S5_TIME_TILE
    rows = bsz * tt
    blk = pl.BlockSpec((bsz, tt, width), lambda i: (0, i, 0))
    return pl.pallas_call(
        _s5_kernel,
        grid=(seq // tt,),
        in_specs=[blk, blk, _full(bre.shape), _full(bim.shape), _full(cre.shape),
                  _full(cimn.shape), _full(lre.shape), _full(lim.shape), _full(dskip.shape),
                  _full(wglu.shape), _full(bglu.shape)],
        out_specs=blk,
        out_shape=jax.ShapeDtypeStruct((bsz, seq, width), F32),
        scratch_shapes=[
            pltpu.VMEM((width // LANES, rows, LANES), F32),
            pltpu.VMEM((rows, width), F32),
            pltpu.VMEM((rows, nstate), F32),
            pltpu.VMEM((rows, nstate), F32),
            pltpu.VMEM((rows, nstate), BF16),
            pltpu.VMEM((rows, nstate), BF16),
            pltpu.VMEM((bsz, nstate), F32),
            pltpu.VMEM((bsz, nstate), F32),
            pltpu.VMEM((width // LANES, rows, LANES), F32),
            pltpu.VMEM((rows, width), F32),
        ],
        compiler_params=_params(),
        name="s5_mixer",
    )(u, gb, bre, bim, cre, cimn, lre, lim, dskip, wglu, bglu)


def _s5_weights(lam_re, lam_im, log_dt, b_re, b_im, c_re, c_im):
    g, n = lam_re.shape
    p = b_re.shape[-1]
    lam = lax.complex(lam_re.astype(F32), lam_im.astype(F32))
    dt = jnp.exp(log_dt.astype(F32))[:, None]
    lam_bar = jnp.exp(lam * dt)
    b_bar = ((lam_bar - 1.0) / lam)[..., None] * lax.complex(b_re.astype(F32), b_im.astype(F32))
    eye = jnp.eye(g, dtype=F32)
    bd_in = lambda m: jnp.einsum('gnp,gh->gphn', m, eye).reshape(g * p, g * n)
    bd_out = lambda m: jnp.einsum('gpn,gh->gnhp', m, eye).reshape(g * n, g * p)
    return (bd_in(b_bar.real).astype(BF16), bd_in(b_bar.imag).astype(BF16),
            bd_out(c_re.astype(F32)).astype(BF16), bd_out(-c_im.astype(F32)).astype(BF16),
            lam_bar.real.reshape(1, g * n), lam_bar.imag.reshape(1, g * n))


def _odd_in_body(x, g_ref, w_ref, wr_ref, wg_ref, bg_ref,
                 q_ref, k_ref, v_ref, gz_ref, la_ref):
    key = GLA_HEADS * GLA_DK
    val = GLA_HEADS * GLA_DV
    h = _rms(x, g_ref[...]).astype(BF16)
    seg = 512

    def proj(lo):
        return _dot(h, w_ref[:, lo:lo + seg])

    q_ref[...] = proj(0) * (GLA_DK ** -0.5)
    k_ref[...] = proj(key)
    for s in range(val // seg):
        v_ref[:, s * seg:(s + 1) * seg] = proj(2 * key + s * seg).astype(BF16)
    for s in range(val // seg):
        gz_ref[:, s * seg:(s + 1) * seg] = _silu(proj(2 * key + val + s * seg)).astype(BF16)
    r = _dot(h, wr_ref[...])
    pre = _dot(r.astype(BF16), wg_ref[...]) + bg_ref[...]
    la_ref[...] = -_softplus(-pre) * (1.0 / GLA_TAU)


def _gla_kernel(q_ref, k_ref, la_ref, v_ref, gz_ref, tri_ref, og_ref, o_ref,
                st_ref, qd_ref, sc_ref, ut_ref):
    c = GLA_CHUNK
    nchunk = q_ref.shape[1] // c
    items = [(i, hd) for i in range(nchunk) for hd in range(GLA_HEADS)]

    @pl.when(pl.program_id(1) == 0)
    def _():
        st_ref[...] = jnp.zeros_like(st_ref)

    tri2 = tri_ref[...]
    row = lax.broadcasted_iota(jnp.int32, (c, c), 0)
    col = lax.broadcasted_iota(jnp.int32, (c, c), 1)
    causal = col <= row
    og = og_ref[...]
    rows = lambda i: slice(i * c, (i + 1) * c)
    ksl = lambda hd: slice(hd * GLA_DK, (hd + 1) * GLA_DK)
    vsl = lambda hd: slice(hd * GLA_DV, (hd + 1) * GLA_DV)

    gs = []
    for i, hd in items:
        hi, lo = _split_bf16(la_ref[0, rows(i), ksl(hd)])
        gs.append(_dot(tri2, jnp.concatenate([hi, lo], axis=0)))
    k_invs, k_decs, e_lasts = [], [], []
    for n, (i, hd) in enumerate(items):
        g = gs[n]
        g_last = g[c - 1:c, :]
        k = k_ref[0, rows(i), ksl(hd)]
        qd_ref[i, hd] = (q_ref[0, rows(i), ksl(hd)] * jnp.exp(g)).astype(BF16)
        k_invs.append((k * jnp.exp(-g)).astype(BF16))
        k_decs.append((k * jnp.exp(g_last - g)).astype(BF16))
        e_lasts.append(jnp.exp(g_last))
    for n, (i, hd) in enumerate(items):
        scores = jnp.where(causal, _dot_nt(qd_ref[i, hd], k_invs[n]), 0.0)
        sc_ref[i, hd] = scores.astype(BF16)
    for n, (i, hd) in enumerate(items):
        ut_ref[i, hd] = lax.dot_general(v_ref[0, rows(i), vsl(hd)], k_decs[n],
                                        (((0,), (0,)), ((), ())), preferred_element_type=F32)

    for n, (i, hd) in enumerate(items):
        st = st_ref[hd]
        o = _dot(sc_ref[i, hd], v_ref[0, rows(i), vsl(hd)]) + _dot_nt(qd_ref[i, hd], st.astype(BF16))
        st_ref[hd] = st * e_lasts[n] + ut_ref[i, hd]
        o = _rms(o, og) * gz_ref[0, rows(i), vsl(hd)].astype(F32)
        o_ref[0, rows(i), vsl(hd)] = o.astype(BF16)


def _gla(q, k, la, v, gz, tri2, og):
    bsz, seq, key = q.shape
    val = v.shape[2]
    tt = GLA_TIME_TILE
    nchunk = tt // GLA_CHUNK
    kspec = pl.BlockSpec((1, tt, key), lambda b, i: (b, i, 0))
    vspec = pl.BlockSpec((1, tt, val), lambda b, i: (b, i, 0))
    return pl.pallas_call(
        _gla_kernel,
        grid=(bsz, seq // tt),
        in_specs=[kspec, kspec, kspec, vspec, vspec, _full(tri2.shape), _full(og.shape)],
        out_specs=vspec,
        out_shape=jax.ShapeDtypeStruct((bsz, seq, val), BF16),
        scratch_shapes=[
            pltpu.VMEM((GLA_HEADS, GLA_DV, GLA_DK), F32),
            pltpu.VMEM((nchunk, GLA_HEADS, GLA_CHUNK, GLA_DK), BF16),
            pltpu.VMEM((nchunk, GLA_HEADS, GLA_CHUNK, GLA_CHUNK), BF16),
            pltpu.VMEM((nchunk, GLA_HEADS, GLA_DV, GLA_DK), F32),
        ],
        compiler_params=_params(2),
        name="gla",
    )(q, k, la, v, gz, tri2, og)


def kernel(x, even_norm_g, even_w_in, sb_q_norm_g, sb_k_norm_g, s5_lambda_re, s5_lambda_im,
           s5_log_dt, s5_b_re, s5_b_im, s5_c_re, s5_c_im, s5_d, s5_w_glu, s5_b_glu, even_w_out,
           odd_norm_g, odd_w_in, gla_w_gate, gla_b_gate, gla_o_norm_g, odd_w_out):
    bsz, seq, d = x.shape
    n = bsz * seq
    depth = even_norm_g.shape[0] + odd_norm_g.shape[0]
    sb_width = SB_HEADS * SB_HEAD_DIM
    key = GLA_HEADS * GLA_DK
    val = GLA_HEADS * GLA_DV
    assert n % ROW_TILE == 0 and seq % (SB_QGROUPS * SB_BLOCK) == 0 and seq % GLA_TIME_TILE == 0
    assert seq % S5_TIME_TILE == 0 and bsz % 8 == 0

    t = SB_BLOCK
    j_idx = jnp.arange(t)[:, None]
    s_idx = jnp.arange(t)[None, :]
    uo = jnp.concatenate([(j_idx > s_idx), jnp.ones((t, t), bool)], axis=1).astype(BF16)
    uo = jnp.concatenate([uo, uo], axis=0)
    c_idx = jnp.arange(GLA_CHUNK)
    tri = (c_idx[None, :] <= c_idx[:, None]).astype(BF16)
    tri = jnp.concatenate([tri, tri], axis=1)

    def in_side(layer):
        i = layer // 2
        if layer >= depth:
            return None, []
        if layer % 2 == 0:
            return "even", [even_norm_g[i].reshape(1, d).astype(F32), even_w_in[i].astype(BF16),
                            sb_q_norm_g[i].reshape(1, -1).astype(F32),
                            sb_k_norm_g[i].reshape(1, -1).astype(F32)]
        w = odd_w_in[i]
        main = 2 * key + 2 * val
        wr = jnp.pad(w[:, main:], ((0, 0), (0, LANES - GLA_RANK))).astype(BF16)
        wg = jnp.pad(gla_w_gate[i], ((0, LANES - GLA_RANK), (0, 0))).astype(BF16)
        return "odd", [odd_norm_g[i].reshape(1, d).astype(F32), w[:, :main].astype(BF16), wr, wg,
                       gla_b_gate[i].reshape(1, key).astype(F32)]

    x2 = x.reshape(n, d).astype(F32)
    x2, proj = _layer_io(x2, [], None, *in_side(0))
    for layer in range(depth):
        i = layer // 2
        if layer % 2 == 0:
            q, k, v, ga, u, gb = proj
            shp = (bsz, seq, sb_width)
            o_a = _sb_attn(q.reshape(shp), k.reshape(shp), v.reshape(shp), ga.reshape(shp), uo)
            s5w = u.shape[1]
            bre, bim, cre, cimn, lre, lim = _s5_weights(
                s5_lambda_re[i], s5_lambda_im[i], s5_log_dt[i], s5_b_re[i], s5_b_im[i],
                s5_c_re[i], s5_c_im[i])
            o_b = _s5(u.reshape(bsz, seq, s5w), gb.reshape(bsz, seq, s5w), bre, bim, cre, cimn,
                      lre, lim, s5_d[i].reshape(1, s5w).astype(F32), s5_w_glu[i].astype(BF16),
                      s5_b_glu[i].reshape(1, s5w).astype(F32))
            acts, w_out = [o_a.reshape(n, sb_width), o_b.reshape(n, s5w)], even_w_out[i]
        else:
            q, k, v, gz, la = proj
            o = _gla(q.reshape(bsz, seq, key), k.reshape(bsz, seq, key), la.reshape(bsz, seq, key),
                     v.reshape(bsz, seq, val), gz.reshape(bsz, seq, val), tri,
                     gla_o_norm_g[i].reshape(1, GLA_DV).astype(F32))
            acts, w_out = [o.reshape(n, val)], odd_w_out[i]
        x2, proj = _layer_io(x2, acts, w_out.astype(BF16), *in_side(layer + 1))
    return x2.reshape(bsz, seq, d).astype(x.dtype)
```

```python
import functools
import math

import jax
import jax.numpy as jnp
from jax import lax
from jax.experimental import pallas as pl
from jax.experimental.pallas import tpu as pltpu

F32 = jnp.float32
BF16 = jnp.bfloat16

EPS = 1e-6
SB_HEADS = 8
SB_HEAD_DIM = 128
S5_GROUP = 16
S5_STATE = 64
GLA_HEADS = 4
GLA_DK = 128
GLA_DV = 256
GLA_RANK = 16
GLA_TAU = 16.0
GLA_CHUNK = 64

LANES = 128
MXU_DIM = 256
VMEM_LIMIT = 56 * 1024 * 1024

ROW_TILE = 512
ROW_PARTS = 2
SB_BLOCK = 128
SB_QGROUPS = 2
SB_DEAD_LOG = 88.0
S5_TIME_TILE = 32
S5_SUB = 8
GLA_TIME_TILE = 1024


def _dot(a, b):
    return jnp.dot(a, b, preferred_element_type=F32)


def _dot_nt(a, b):
    return lax.dot_general(a, b, (((1,), (1,)), ((), ())), preferred_element_type=F32)


def _rms(x, g):
    return x * lax.rsqrt(jnp.mean(x * x, axis=-1, keepdims=True) + EPS) * g


def _sigmoid(x):
    return 1.0 / (1.0 + jnp.exp(-x))


def _silu(x):
    return x * _sigmoid(x)


def _softplus(x):
    neg_abs = pltpu.bitcast(pltpu.bitcast(x, jnp.uint32) | jnp.uint32(0x80000000), F32)
    return jnp.maximum(x, 0.0) + jnp.log(1.0 + jnp.exp(neg_abs))


def _split_bf16(x):
    hi = x.astype(BF16)
    lo = (x - hi.astype(F32)).astype(BF16)
    return hi, lo


def _params(n_axes=1):
    return pltpu.CompilerParams(dimension_semantics=("arbitrary",) * n_axes,
                                vmem_limit_bytes=VMEM_LIMIT)


def _full(shape):
    return pl.BlockSpec(shape, lambda *_: (0,) * len(shape))


def _even_in_body(xs, rows, g_ref, w_ref, qg_ref, kg_ref,
                  q_ref, k_ref, v_ref, ga_ref, u_ref, gb_ref):
    width = SB_HEADS * SB_HEAD_DIM
    hs = [_rms(x, g_ref[...]).astype(BF16) for x in xs]
    parts = range(len(xs))
    seg = 512

    def proj(r, lo):
        return _dot(hs[r], w_ref[:, lo:lo + seg])

    for ref, gain_ref, off, scale in ((q_ref, qg_ref, 0, SB_HEAD_DIM ** -0.5),
                                      (k_ref, kg_ref, width, 1.0)):
        gain = gain_ref[...] * scale
        for s in range(width // seg):
            for r in parts:
                p = proj(r, off + s * seg)
                for hd in range(seg // SB_HEAD_DIM):
                    ph = p[:, hd * SB_HEAD_DIM:(hd + 1) * SB_HEAD_DIM]
                    lo = s * seg + hd * SB_HEAD_DIM
                    ref[rows[r], lo:lo + SB_HEAD_DIM] = _rms(ph, gain).astype(BF16)
    for s in range(width // seg):
        for r in parts:
            v_ref[rows[r], s * seg:(s + 1) * seg] = proj(r, 2 * width + s * seg).astype(BF16)
    for s in range(width // seg):
        for r in parts:
            ga_ref[rows[r], s * seg:(s + 1) * seg] = _silu(proj(r, 3 * width + s * seg)).astype(BF16)
    for r in parts:
        gb_ref[rows[r], :] = _silu(proj(r, 4 * width + seg))
    for r in parts:
        u_ref[rows[r], :] = proj(r, 4 * width)


def _io_kernel(*refs, n_acts, in_kind):
    x_ref, refs = refs[0], refs[1:]
    tm = x_ref.shape[0]
    rows = [slice(r * tm // ROW_PARTS, (r + 1) * tm // ROW_PARTS) for r in range(ROW_PARTS)]
    xs = [x_ref[rw, :] for rw in rows]
    n_in = {None: 0, "even": 4, "odd": 5}[in_kind]
    if n_acts:
        act_refs, wout_ref, refs = refs[:n_acts], refs[n_acts], refs[n_acts + 1:]
        for r, rw in enumerate(rows):
            off = 0
            for a_ref in act_refs:
                kdim = a_ref.shape[1]
                xs[r] = xs[r] + _dot(a_ref[rw, :].astype(BF16), wout_ref[off:off + kdim, :])
                off += kdim
            refs[n_in][rw, :] = xs[r]
        in_refs, out_refs = refs[:n_in], refs[n_in + 1:]
    else:
        in_refs, out_refs = refs[:n_in], refs[n_in:]
    if in_kind == "even":
        _even_in_body(xs, rows, *in_refs, *out_refs)
    elif in_kind == "odd":
        _odd_in_body(xs, rows, *in_refs, *out_refs)


def _layer_io(x2, acts, w_out, in_kind, in_params):
    n, d = x2.shape
    row = lambda c: pl.BlockSpec((ROW_TILE, c), lambda i: (i, 0))
    once = lambda a: pl.BlockSpec(a.shape, lambda i: (0,) * a.ndim, pipeline_mode=pl.Buffered(1))
    sds = jax.ShapeDtypeStruct
    args, in_specs, out_specs, out_shape = [x2], [row(d)], [], []
    if acts:
        args += list(acts) + [w_out]
        in_specs += [row(a.shape[1]) for a in acts] + [once(w_out)]
        out_specs.append(row(d))
        out_shape.append(sds((n, d), F32))
    args += list(in_params)
    in_specs += [once(p) for p in in_params]
    if in_kind == "even":
        width = SB_HEADS * SB_HEAD_DIM
        s5w = (in_params[1].shape[1] - 4 * width) // 2
        outs = [(width, BF16)] * 4 + [(s5w, F32)] * 2
    elif in_kind == "odd":
        key, val = GLA_HEADS * GLA_DK, GLA_HEADS * GLA_DV
        outs = [(key, F32), (key, F32), (val, BF16), (val, BF16), (key, F32)]
    else:
        outs = []
    out_specs += [row(c) for c, _ in outs]
    out_shape += [sds((n, c), dt) for c, dt in outs]
    res = pl.pallas_call(
        functools.partial(_io_kernel, n_acts=len(acts), in_kind=in_kind),
        grid=(n // ROW_TILE,),
        in_specs=in_specs,
        out_specs=out_specs,
        out_shape=out_shape,
        compiler_params=_params(),
        name=("out_" if acts else "") + (in_kind + "_in_" if in_kind else "") + "proj",
    )(*args)
    return (res[0], res[1:]) if acts else (x2, res)


def _sb_attn_kernel(q_ref, k_ref, v_ref, gate_ref, uo_ref, o_ref, carry_ref, acc_ref):
    t = SB_BLOCK
    ng = SB_QGROUPS
    heads = q_ref.shape[2] // SB_HEAD_DIM
    base = pl.program_id(1) * ng
    uo = uo_ref[...]
    row = lax.broadcasted_iota(jnp.int32, (t, t), 0)
    col = lax.broadcasted_iota(jnp.int32, (t, t), 1)
    strict = col < row
    lanes = [slice(hd * SB_HEAD_DIM, (hd + 1) * SB_HEAD_DIM) for hd in range(heads)]
    qrows = [slice(g * t, (g + 1) * t) for g in range(ng)]

    def sweep(blocks):
        krows = {(g, n): pl.ds(pl.multiple_of(kb * t, t), t)
                 for g, blk in blocks.items() for n, (kb, _) in enumerate(blk)}
        items = [(g, n, hd) for (g, n) in krows for hd in range(heads)]
        zs = {(g, n, hd): _dot_nt(q_ref[0, qrows[g], lanes[hd]], k_ref[0, krows[g, n], lanes[hd]])
              for g, n, hd in items}
        sps = {it: _softplus(zs[it]) for it in items}
        r2s = {}
        for g, n, hd in items:
            sp = sps[g, n, hd]
            hi, lo = _split_bf16(jnp.where(strict, sp, 0.0) if blocks[g][n][1] else sp)
            r2s[g, n, hd] = _dot(jnp.concatenate([hi, lo], axis=1), uo)
        low = None
        for g, blk in blocks.items():
            fresh = blk[0][1]
            for hd in range(heads):
                carry = None if fresh else carry_ref[g, hd]
                pv = None
                for n, (_, diag) in enumerate(blk):
                    r2 = r2s[g, n, hd]
                    rest = r2[:, :t] if carry is None else r2[:, :t] + carry
                    carry = r2[:, t:] if carry is None else r2[:, t:] + carry
                    w = jnp.exp(zs[g, n, hd] - sps[g, n, hd] - rest)
                    if diag:
                        w = jnp.where(strict, w, 0.0)
                    d = _dot(w.astype(BF16), v_ref[0, krows[g, n], lanes[hd]])
                    pv = d if pv is None else pv + d
                carry_ref[g, hd] = carry
                low = carry if low is None else jnp.minimum(low, carry)
                acc_ref[g, hd] = pv if fresh else acc_ref[g, hd] + pv
        return jnp.min(low) < SB_DEAD_LOG

    def first_step():
        sweep({g: [(g - n, n == 0) for n in range(g + 1)] for g in range(ng)})
        return jnp.bool_(False)

    def later_step():
        return sweep({g: [(base + g, True), (base + g - 1, False)] for g in range(ng)})

    def cond(state):
        j, live = state
        return jnp.logical_and(j <= base, live)

    def body(state):
        j, _ = state
        return j + 1, sweep({g: [(base + g - j, False)] for g in range(ng)})

    _, live = lax.while_loop(cond, body, (jnp.int32(2), lax.cond(base == 0, first_step, later_step)))

    for extra in range(1, ng):
        @pl.when(live)
        def _():
            sweep({g: [(g - extra, False)] for g in range(extra, ng)})

    for g in range(ng):
        for hd in range(heads):
            o_ref[0, qrows[g], lanes[hd]] = (
                acc_ref[g, hd] * gate_ref[0, qrows[g], lanes[hd]].astype(F32)).astype(BF16)


def _sb_attn(q, k, v, gate, uo):
    bsz, seq, width = q.shape
    t = SB_BLOCK
    tq = SB_QGROUPS * t
    heads = width // SB_HEAD_DIM
    qspec = pl.BlockSpec((1, tq, width), lambda b, i: (b, i, 0))
    kspec = pl.BlockSpec((1, seq, width), lambda b, i: (b, 0, 0))
    return pl.pallas_call(
        _sb_attn_kernel,
        grid=(bsz, seq // tq),
        in_specs=[qspec, kspec, kspec, qspec, _full(uo.shape)],
        out_specs=qspec,
        out_shape=jax.ShapeDtypeStruct((bsz, seq, width), BF16),
        scratch_shapes=[pltpu.VMEM((SB_QGROUPS, heads, t, t), F32),
                        pltpu.VMEM((SB_QGROUPS, heads, t, SB_HEAD_DIM), F32)],
        compiler_params=_params(2),
        name="sb_attention",
    )(q, k, v, gate, uo)


def _gelu_tanh(x):
    c = math.sqrt(2.0 / math.pi)
    return 0.5 * x * (1.0 + jnp.tanh(c * (x + 0.044715 * (x * x * x))))


def _s5_kernel(u_ref, gb_ref, bre_ref, bim_ref, cre_ref, cimn_ref, lre_ref, lim_ref,
               d_ref, wglu_ref, bglu_ref, o_ref,
               us_ref, ut_ref, bure_ref, buim_ref, hre_ref, him_ref,
               hsre_ref, hsim_ref, ys_ref, yn_ref):
    bsz, tt, width = u_ref.shape
    rows = bsz * tt
    nstate = bure_ref.shape[1]
    nsub = tt // S5_SUB
    nslab = width // LANES

    @pl.when(pl.program_id(0) == 0)
    def _():
        hsre_ref[...] = jnp.zeros_like(hsre_ref)
        hsim_ref[...] = jnp.zeros_like(hsim_ref)

    for c in range(nslab):
        for b in range(bsz):
            for sub in range(nsub):
                r0 = sub * bsz * S5_SUB + b * S5_SUB
                us_ref[c, r0:r0 + S5_SUB, :] = u_ref[b, sub * S5_SUB:(sub + 1) * S5_SUB,
                                                     c * LANES:(c + 1) * LANES]
    for sub in range(nsub):
        for t8 in range(S5_SUB):
            step = sub * S5_SUB + t8
            for c in range(nslab):
                ut_ref[step * bsz:(step + 1) * bsz, c * LANES:(c + 1) * LANES] = (
                    us_ref[c, pl.ds(sub * bsz * S5_SUB + t8, bsz, stride=S5_SUB), :])

    ut = ut_ref[...].astype(BF16)
    groups_per_tile = MXU_DIM // S5_STATE
    in_per_tile = groups_per_tile * S5_GROUP
    for j in range(nstate // MXU_DIM):
        kc = (j * in_per_tile) // MXU_DIM
        lhs = ut[:, kc * MXU_DIM:(kc + 1) * MXU_DIM]
        ksl = slice(kc * MXU_DIM, (kc + 1) * MXU_DIM)
        nsl = slice(j * MXU_DIM, (j + 1) * MXU_DIM)
        bure_ref[:, nsl] = _dot(lhs, bre_ref[ksl, nsl])
        buim_ref[:, nsl] = _dot(lhs, bim_ref[ksl, nsl])

    chunk = 512
    for c in range(nstate // chunk):
        sl = slice(c * chunk, (c + 1) * chunk)
        lr = jnp.broadcast_to(lre_ref[:, sl], (bsz, chunk))
        li = jnp.broadcast_to(lim_ref[:, sl], (bsz, chunk))
        hr = hsre_ref[:, sl]
        hi = hsim_ref[:, sl]
        for step in range(tt):
            rsl = slice(step * bsz, (step + 1) * bsz)
            hr, hi = (lr * hr - li * hi + bure_ref[rsl, sl],
                      lr * hi + li * hr + buim_ref[rsl, sl])
            hre_ref[rsl, sl] = hr.astype(BF16)
            him_ref[rsl, sl] = hi.astype(BF16)
        hsre_ref[:, sl] = hr
        hsim_ref[:, sl] = hi

    tiles_per_slab = (LANES // S5_GROUP) * S5_STATE // MXU_DIM
    for m in range(nslab):
        acc = None
        nsl = slice(m * LANES, (m + 1) * LANES)
        for k in range(m * tiles_per_slab, (m + 1) * tiles_per_slab):
            ksl = slice(k * MXU_DIM, (k + 1) * MXU_DIM)
            d = _dot(hre_ref[:, ksl], cre_ref[ksl, nsl]) + _dot(him_ref[:, ksl], cimn_ref[ksl, nsl])
            acc = d if acc is None else acc + d
        ys_ref[m] = acc
    for b in range(bsz):
        for c in range(nslab):
            yn_ref[b * tt:(b + 1) * tt, c * LANES:(c + 1) * LANES] = (
                ys_ref[c, pl.ds(b, tt, stride=bsz), :])

    u_nat = u_ref[...].reshape(rows, width)
    y = _gelu_tanh(yn_ref[...] + d_ref[...] * u_nat)
    glu = _dot(y.astype(BF16), wglu_ref[...]) + bglu_ref[...]
    out = y * _sigmoid(glu) * gb_ref[...].reshape(rows, width)
    o_ref[...] = out.reshape(bsz, tt, width)


def _s5(u, gb, bre, bim, cre, cimn, lre, lim, dskip, wglu, bglu):
    bsz, seq, width = u.shape
    nstate = bre.shape[1]
    tt = S5_TIME_TILE
    rows = bsz * tt
    blk = pl.BlockSpec((bsz, tt, width), lambda i: (0, i, 0))
    return pl.pallas_call(
        _s5_kernel,
        grid=(seq // tt,),
        in_specs=[blk, blk, _full(bre.shape), _full(bim.shape), _full(cre.shape),
                  _full(cimn.shape), _full(lre.shape), _full(lim.shape), _full(dskip.shape),
                  _full(wglu.shape), _full(bglu.shape)],
        out_specs=blk,
        out_shape=jax.ShapeDtypeStruct((bsz, seq, width), F32),
        scratch_shapes=[
            pltpu.VMEM((width // LANES, rows, LANES), F32),
            pltpu.VMEM((rows, width), F32),
            pltpu.VMEM((rows, nstate), F32),
            pltpu.VMEM((rows, nstate), F32),
            pltpu.VMEM((rows, nstate), BF16),
            pltpu.VMEM((rows, nstate), BF16),
            pltpu.VMEM((bsz, nstate), F32),
            pltpu.VMEM((bsz, nstate), F32),
            pltpu.VMEM((width // LANES, rows, LANES), F32),
            pltpu.VMEM((rows, width), F32),
        ],
        compiler_params=_params(),
        name="s5_mixer",
    )(u, gb, bre, bim, cre, cimn, lre, lim, dskip, wglu, bglu)


def _s5_weights(lam_re, lam_im, log_dt, b_re, b_im, c_re, c_im):
    g, n = lam_re.shape
    p = b_re.shape[-1]
    lam = lax.complex(lam_re.astype(F32), lam_im.astype(F32))
    dt = jnp.exp(log_dt.astype(F32))[:, None]
    lam_bar = jnp.exp(lam * dt)
    b_bar = ((lam_bar - 1.0) / lam)[..., None] * lax.complex(b_re.astype(F32), b_im.astype(F32))
    eye = jnp.eye(g, dtype=F32)
    bd_in = lambda m: jnp.einsum('gnp,gh->gphn', m, eye).reshape(g * p, g * n)
    bd_out = lambda m: jnp.einsum('gpn,gh->gnhp', m, eye).reshape(g * n, g * p)
    return (bd_in(b_bar.real).astype(BF16), bd_in(b_bar.imag).astype(BF16),
            bd_out(c_re.astype(F32)).astype(BF16), bd_out(-c_im.astype(F32)).astype(BF16),
            lam_bar.real.reshape(1, g * n), lam_bar.imag.reshape(1, g * n))


def _odd_in_body(xs, rows, g_ref, w_ref, wr_ref, wg_ref, bg_ref,
                 q_ref, k_ref, v_ref, gz_ref, la_ref):
    key = GLA_HEADS * GLA_DK
    val = GLA_HEADS * GLA_DV
    hs = [_rms(x, g_ref[...]).astype(BF16) for x in xs]
    parts = range(len(xs))
    seg = 512

    def proj(r, lo):
        return _dot(hs[r], w_ref[:, lo:lo + seg])

    codes = [_dot(hs[r], wr_ref[...]).astype(BF16) for r in parts]
    for r in parts:
        q_ref[rows[r], :] = proj(r, 0) * (GLA_DK ** -0.5)
    for r in parts:
        k_ref[rows[r], :] = proj(r, key)
    for r in parts:
        pre = _dot(codes[r], wg_ref[...]) + bg_ref[...]
        la_ref[rows[r], :] = -_softplus(-pre) * (1.0 / GLA_TAU)
    for s in range(val // seg):
        for r in parts:
            gz_ref[rows[r], s * seg:(s + 1) * seg] = _silu(proj(r, 2 * key + val + s * seg)).astype(BF16)
    for s in range(val // seg):
        for r in parts:
            v_ref[rows[r], s * seg:(s + 1) * seg] = proj(r, 2 * key + s * seg).astype(BF16)


def _gla_kernel(q_ref, k_ref, la_ref, v_ref, gz_ref, tri_ref, og_ref, o_ref,
                st_ref, qd_ref, sc_ref, ut_ref):
    c = GLA_CHUNK
    nchunk = q_ref.shape[1] // c
    items = [(i, hd) for i in range(nchunk) for hd in range(GLA_HEADS)]

    @pl.when(pl.program_id(1) == 0)
    def _():
        st_ref[...] = jnp.zeros_like(st_ref)

    tri2 = tri_ref[...]
    row = lax.broadcasted_iota(jnp.int32, (c, c), 0)
    col = lax.broadcasted_iota(jnp.int32, (c, c), 1)
    causal = col <= row
    og = og_ref[...]
    rows = lambda i: slice(i * c, (i + 1) * c)
    ksl = lambda hd: slice(hd * GLA_DK, (hd + 1) * GLA_DK)
    vsl = lambda hd: slice(hd * GLA_DV, (hd + 1) * GLA_DV)

    gs = []
    for i, hd in items:
        hi, lo = _split_bf16(la_ref[0, rows(i), ksl(hd)])
        gs.append(_dot(tri2, jnp.concatenate([hi, lo], axis=0)))
    k_invs, k_decs, e_lasts = [], [], []
    for n, (i, hd) in enumerate(items):
        g = gs[n]
        g_last = g[c - 1:c, :]
        k = k_ref[0, rows(i), ksl(hd)]
        qd_ref[i, hd] = (q_ref[0, rows(i), ksl(hd)] * jnp.exp(g)).astype(BF16)
        k_invs.append((k * jnp.exp(-g)).astype(BF16))
        k_decs.append((k * jnp.exp(g_last - g)).astype(BF16))
        e_lasts.append(jnp.exp(g_last))
    for n, (i, hd) in enumerate(items):
        scores = jnp.where(causal, _dot_nt(qd_ref[i, hd], k_invs[n]), 0.0)
        sc_ref[i, hd] = scores.astype(BF16)
    for n, (i, hd) in enumerate(items):
        ut_ref[i, hd] = lax.dot_general(v_ref[0, rows(i), vsl(hd)], k_decs[n],
                                        (((0,), (0,)), ((), ())), preferred_element_type=F32)

    for n, (i, hd) in enumerate(items):
        st = st_ref[hd]
        o = _dot(sc_ref[i, hd], v_ref[0, rows(i), vsl(hd)]) + _dot_nt(qd_ref[i, hd], st.astype(BF16))
        st_ref[hd] = st * e_lasts[n] + ut_ref[i, hd]
        o = _rms(o, og) * gz_ref[0, rows(i), vsl(hd)].astype(F32)
        o_ref[0, rows(i), vsl(hd)] = o.astype(BF16)


def _gla(q, k, la, v, gz, tri2, og):
    bsz, seq, key = q.shape
    val = v.shape[2]
    tt = GLA_TIME_TILE
    nchunk = tt // GLA_CHUNK
    kspec = pl.BlockSpec((1, tt, key), lambda b, i: (b, i, 0))
    vspec = pl.BlockSpec((1, tt, val), lambda b, i: (b, i, 0))
    return pl.pallas_call(
        _gla_kernel,
        grid=(bsz, seq // tt),
        in_specs=[kspec, kspec, kspec, vspec, vspec, _full(tri2.shape), _full(og.shape)],
        out_specs=vspec,
        out_shape=jax.ShapeDtypeStruct((bsz, seq, val), BF16),
        scratch_shapes=[
            pltpu.VMEM((GLA_HEADS, GLA_DV, GLA_DK), F32),
            pltpu.VMEM((nchunk, GLA_HEADS, GLA_CHUNK, GLA_DK), BF16),
            pltpu.VMEM((nchunk, GLA_HEADS, GLA_CHUNK, GLA_CHUNK), BF16),
            pltpu.VMEM((nchunk, GLA_HEADS, GLA_DV, GLA_DK), F32),
        ],
        compiler_params=_params(2),
        name="gla",
    )(q, k, la, v, gz, tri2, og)


def kernel(x, even_norm_g, even_w_in, sb_q_norm_g, sb_k_norm_g, s5_lambda_re, s5_lambda_im,
           s5_log_dt, s5_b_re, s5_b_im, s5_c_re, s5_c_im, s5_d, s5_w_glu, s5_b_glu, even_w_out,
           odd_norm_g, odd_w_in, gla_w_gate, gla_b_gate, gla_o_norm_g, odd_w_out):
    bsz, seq, d = x.shape
    n = bsz * seq
    depth = even_norm_g.shape[0] + odd_norm_g.shape[0]
    sb_width = SB_HEADS * SB_HEAD_DIM
    key = GLA_HEADS * GLA_DK
    val = GLA_HEADS * GLA_DV
    assert n % ROW_TILE == 0 and seq % (SB_QGROUPS * SB_BLOCK) == 0 and seq % GLA_TIME_TILE == 0
    assert seq % S5_TIME_TILE == 0 and bsz % 8 == 0

    t = SB_BLOCK
    j_idx = jnp.arange(t)[:, None]
    s_idx = jnp.arange(t)[None, :]
    uo = jnp.concatenate([(j_idx > s_idx), jnp.ones((t, t), bool)], axis=1).astype(BF16)
    uo = jnp.concatenate([uo, uo], axis=0)
    c_idx = jnp.arange(GLA_CHUNK)
    tri = (c_idx[None, :] <= c_idx[:, None]).astype(BF16)
    tri = jnp.concatenate([tri, tri], axis=1)

    def in_side(layer):
        i = layer // 2
        if layer >= depth:
            return None, []
        if layer % 2 == 0:
            return "even", [even_norm_g[i].reshape(1, d).astype(F32), even_w_in[i].astype(BF16),
                            sb_q_norm_g[i].reshape(1, -1).astype(F32),
                            sb_k_norm_g[i].reshape(1, -1).astype(F32)]
        w = odd_w_in[i]
        main = 2 * key + 2 * val
        wr = jnp.pad(w[:, main:], ((0, 0), (0, LANES - GLA_RANK))).astype(BF16)
        wg = jnp.pad(gla_w_gate[i], ((0, LANES - GLA_RANK), (0, 0))).astype(BF16)
        return "odd", [odd_norm_g[i].reshape(1, d).astype(F32), w[:, :main].astype(BF16), wr, wg,
                       gla_b_gate[i].reshape(1, key).astype(F32)]

    s5_all = jax.vmap(_s5_weights)(s5_lambda_re, s5_lambda_im, s5_log_dt, s5_b_re, s5_b_im,
                                   s5_c_re, s5_c_im)

    x2 = x.reshape(n, d).astype(F32)
    x2, proj = _layer_io(x2, [], None, *in_side(0))
    for layer in range(depth):
        i = layer // 2
        if layer % 2 == 0:
            q, k, v, ga, u, gb = proj
            shp = (bsz, seq, sb_width)
            o_a = _sb_attn(q.reshape(shp), k.reshape(shp), v.reshape(shp), ga.reshape(shp), uo)
            s5w = u.shape[1]
            bre, bim, cre, cimn, lre, lim = [a[i] for a in s5_all]
            o_b = _s5(u.reshape(bsz, seq, s5w), gb.reshape(bsz, seq, s5w), bre, bim, cre, cimn,
                      lre, lim, s5_d[i].reshape(1, s5w).astype(F32), s5_w_glu[i].astype(BF16),
                      s5_b_glu[i].reshape(1, s5w).astype(F32))
            acts, w_out = [o_a.reshape(n, sb_width), o_b.reshape(n, s5w)], even_w_out[i]
        else:
            q, k, v, gz, la = proj
            o = _gla(q.reshape(bsz, seq, key), k.reshape(bsz, seq, key), la.reshape(bsz, seq, key),
                     v.reshape(bsz, seq, val), gz.reshape(bsz, seq, val), tri,
                     gla_o_norm_g[i].reshape(1, GLA_DV).astype(F32))
            acts, w_out = [o.reshape(n, val)], odd_w_out[i]
        x2, proj = _layer_io(x2, acts, w_out.astype(BF16), *in_side(layer + 1))
    return x2.reshape(bsz, seq, d).astype(x.dtype)
```

```python
import functools
import math

import jax
import jax.numpy as jnp
from jax import lax
from jax.experimental import pallas as pl
from jax.experimental.pallas import tpu as pltpu

F32 = jnp.float32
BF16 = jnp.bfloat16

EPS = 1e-6
SB_HEADS = 8
SB_HEAD_DIM = 128
S5_GROUP = 16
S5_STATE = 64
GLA_HEADS = 4
GLA_DK = 128
GLA_DV = 256
GLA_RANK = 16
GLA_TAU = 16.0
GLA_CHUNK = 64

LANES = 128
MXU_DIM = 256
VMEM_LIMIT = 56 * 1024 * 1024

ROW_TILE = 512
ROW_PARTS = 2
SB_BLOCK = 128
SB_QGROUPS = 2
SB_DEAD_LOG2 = 127.0
LOG2E = 1.4426950408889634
S5_TIME_TILE = 32
S5_SUB = 8
GLA_TIME_TILE = 1024


def _dot(a, b):
    return jnp.dot(a, b, preferred_element_type=F32)


def _dot_nt(a, b):
    return lax.dot_general(a, b, (((1,), (1,)), ((), ())), preferred_element_type=F32)


def _rms(x, g):
    return x * lax.rsqrt(jnp.mean(x * x, axis=-1, keepdims=True) + EPS) * g


def _sigmoid(x):
    return 1.0 / (1.0 + jnp.exp(-x))


def _silu(x):
    return x * _sigmoid(x)


def _softplus(x):
    neg_abs = pltpu.bitcast(pltpu.bitcast(x, jnp.uint32) | jnp.uint32(0x80000000), F32)
    return jnp.maximum(x, 0.0) + jnp.log(1.0 + jnp.exp(neg_abs))


def _softplus2(x):
    neg_abs = pltpu.bitcast(pltpu.bitcast(x, jnp.uint32) | jnp.uint32(0x80000000), F32)
    return jnp.maximum(x, 0.0) + jnp.log(1.0 + jnp.exp2(neg_abs)) * LOG2E


def _split_bf16(x):
    hi = x.astype(BF16)
    lo = (x - hi.astype(F32)).astype(BF16)
    return hi, lo


def _params(n_axes=1):
    return pltpu.CompilerParams(dimension_semantics=("arbitrary",) * n_axes,
                                vmem_limit_bytes=VMEM_LIMIT)


def _full(shape):
    return pl.BlockSpec(shape, lambda *_: (0,) * len(shape))


def _even_in_body(xs, rows, g_ref, w_ref, qg_ref, kg_ref,
                  q_ref, k_ref, v_ref, ga_ref, u_ref, gb_ref):
    width = SB_HEADS * SB_HEAD_DIM
    hs = [_rms(x, g_ref[...]).astype(BF16) for x in xs]
    parts = range(len(xs))
    seg = 512

    def proj(r, lo):
        return _dot(hs[r], w_ref[:, lo:lo + seg])

    for ref, gain_ref, off, scale in ((q_ref, qg_ref, 0, SB_HEAD_DIM ** -0.5 * LOG2E),
                                      (k_ref, kg_ref, width, 1.0)):
        gain = gain_ref[...] * scale
        for s in range(width // seg):
            for r in parts:
                p = proj(r, off + s * seg)
                for hd in range(seg // SB_HEAD_DIM):
                    ph = p[:, hd * SB_HEAD_DIM:(hd + 1) * SB_HEAD_DIM]
                    lo = s * seg + hd * SB_HEAD_DIM
                    ref[rows[r], lo:lo + SB_HEAD_DIM] = _rms(ph, gain).astype(BF16)
    for s in range(width // seg):
        for r in parts:
            v_ref[rows[r], s * seg:(s + 1) * seg] = proj(r, 2 * width + s * seg).astype(BF16)
    for s in range(width // seg):
        for r in parts:
            ga_ref[rows[r], s * seg:(s + 1) * seg] = _silu(proj(r, 3 * width + s * seg)).astype(BF16)
    for r in parts:
        gb_ref[rows[r], :] = _silu(proj(r, 4 * width + seg))
    for r in parts:
        u_ref[rows[r], :] = proj(r, 4 * width)


def _io_kernel(*refs, n_acts, in_kind):
    x_ref, refs = refs[0], refs[1:]
    tm = x_ref.shape[0]
    rows = [slice(r * tm // ROW_PARTS, (r + 1) * tm // ROW_PARTS) for r in range(ROW_PARTS)]
    xs = [x_ref[rw, :] for rw in rows]
    n_in = {None: 0, "even": 4, "odd": 5}[in_kind]
    if n_acts:
        act_refs, wout_ref, refs = refs[:n_acts], refs[n_acts], refs[n_acts + 1:]
        for r, rw in enumerate(rows):
            off = 0
            for a_ref in act_refs:
                kdim = a_ref.shape[1]
                xs[r] = xs[r] + _dot(a_ref[rw, :].astype(BF16), wout_ref[off:off + kdim, :])
                off += kdim
            refs[n_in][rw, :] = xs[r]
        in_refs, out_refs = refs[:n_in], refs[n_in + 1:]
    else:
        in_refs, out_refs = refs[:n_in], refs[n_in:]
    if in_kind == "even":
        _even_in_body(xs, rows, *in_refs, *out_refs)
    elif in_kind == "odd":
        _odd_in_body(xs, rows, *in_refs, *out_refs)


def _layer_io(x2, acts, w_out, in_kind, in_params):
    n, d = x2.shape
    row = lambda c: pl.BlockSpec((ROW_TILE, c), lambda i: (i, 0))
    once = lambda a: pl.BlockSpec(a.shape, lambda i: (0,) * a.ndim, pipeline_mode=pl.Buffered(1))
    sds = jax.ShapeDtypeStruct
    args, in_specs, out_specs, out_shape = [x2], [row(d)], [], []
    if acts:
        args += list(acts) + [w_out]
        in_specs += [row(a.shape[1]) for a in acts] + [once(w_out)]
        out_specs.append(row(d))
        out_shape.append(sds((n, d), F32))
    args += list(in_params)
    in_specs += [once(p) for p in in_params]
    if in_kind == "even":
        width = SB_HEADS * SB_HEAD_DIM
        s5w = (in_params[1].shape[1] - 4 * width) // 2
        outs = [(width, BF16)] * 4 + [(s5w, F32)] * 2
    elif in_kind == "odd":
        key, val = GLA_HEADS * GLA_DK, GLA_HEADS * GLA_DV
        outs = [(key, F32), (key, F32), (val, BF16), (val, BF16), (key, F32)]
    else:
        outs = []
    out_specs += [row(c) for c, _ in outs]
    out_shape += [sds((n, c), dt) for c, dt in outs]
    res = pl.pallas_call(
        functools.partial(_io_kernel, n_acts=len(acts), in_kind=in_kind),
        grid=(n // ROW_TILE,),
        in_specs=in_specs,
        out_specs=out_specs,
        out_shape=out_shape,
        compiler_params=_params(),
        name=("out_" if acts else "") + (in_kind + "_in_" if in_kind else "") + "proj",
    )(*args)
    return (res[0], res[1:]) if acts else (x2, res)


def _sb_attn_kernel(q_ref, k_ref, v_ref, gate_ref, uo_ref, o_ref, carry_ref, acc_ref):
    t = SB_BLOCK
    ng = SB_QGROUPS
    heads = q_ref.shape[2] // SB_HEAD_DIM
    base = pl.program_id(1) * ng
    uo = uo_ref[...]
    row = lax.broadcasted_iota(jnp.int32, (t, t), 0)
    col = lax.broadcasted_iota(jnp.int32, (t, t), 1)
    strict = col < row
    lanes = [slice(hd * SB_HEAD_DIM, (hd + 1) * SB_HEAD_DIM) for hd in range(heads)]
    qrows = [slice(g * t, (g + 1) * t) for g in range(ng)]

    def sweep(blocks):
        krows = {(g, n): pl.ds(pl.multiple_of(kb * t, t), t)
                 for g, blk in blocks.items() for n, (kb, _) in enumerate(blk)}
        items = [(g, n, hd) for (g, n) in krows for hd in range(heads)]
        zs = {(g, n, hd): _dot_nt(q_ref[0, qrows[g], lanes[hd]], k_ref[0, krows[g, n], lanes[hd]])
              for g, n, hd in items}
        r2s = {}
        for g, n, hd in items:
            sp = _softplus2(zs[g, n, hd])
            hi, lo = _split_bf16(jnp.where(strict, sp, 0.0) if blocks[g][n][1] else sp)
            r2s[g, n, hd] = _dot(jnp.concatenate([hi, lo], axis=1), uo)
        low = None
        for g, blk in blocks.items():
            fresh = blk[0][1]
            for hd in range(heads):
                carry = None if fresh else carry_ref[g, hd]
                pv = None
                for n, (_, diag) in enumerate(blk):
                    r2 = r2s[g, n, hd]
                    rest = r2[:, :t] if carry is None else r2[:, :t] + carry
                    carry = r2[:, t:] if carry is None else r2[:, t:] + carry
                    w = jnp.exp2(zs[g, n, hd] - rest)
                    if diag:
                        w = jnp.where(strict, w, 0.0)
                    d = _dot(w.astype(BF16), v_ref[0, krows[g, n], lanes[hd]])
                    pv = d if pv is None else pv + d
                carry_ref[g, hd] = carry
                low = carry if low is None else jnp.minimum(low, carry)
                acc_ref[g, hd] = pv if fresh else acc_ref[g, hd] + pv
        return jnp.min(low) < SB_DEAD_LOG2

    def first_step():
        sweep({g: [(g - n, n == 0) for n in range(g + 1)] for g in range(ng)})
        return jnp.bool_(False)

    def later_step():
        return sweep({g: [(base + g, True), (base + g - 1, False)] for g in range(ng)})

    def cond(state):
        j, live = state
        return jnp.logical_and(j <= base, live)

    def body(state):
        j, _ = state
        return j + 1, sweep({g: [(base + g - j, False)] for g in range(ng)})

    _, live = lax.while_loop(cond, body, (jnp.int32(2), lax.cond(base == 0, first_step, later_step)))

    for extra in range(1, ng):
        @pl.when(live)
        def _():
            sweep({g: [(g - extra, False)] for g in range(extra, ng)})

    for g in range(ng):
        for hd in range(heads):
            o_ref[0, qrows[g], lanes[hd]] = (
                acc_ref[g, hd] * gate_ref[0, qrows[g], lanes[hd]].astype(F32)).astype(BF16)


def _sb_attn(q, k, v, gate, uo):
    bsz, seq, width = q.shape
    t = SB_BLOCK
    tq = SB_QGROUPS * t
    heads = width // SB_HEAD_DIM
    qspec = pl.BlockSpec((1, tq, width), lambda b, i: (b, i, 0))
    kspec = pl.BlockSpec((1, seq, width), lambda b, i: (b, 0, 0))
    return pl.pallas_call(
        _sb_attn_kernel,
        grid=(bsz, seq // tq),
        in_specs=[qspec, kspec, kspec, qspec, _full(uo.shape)],
        out_specs=qspec,
        out_shape=jax.ShapeDtypeStruct((bsz, seq, width), BF16),
        scratch_shapes=[pltpu.VMEM((SB_QGROUPS, heads, t, t), F32),
                        pltpu.VMEM((SB_QGROUPS, heads, t, SB_HEAD_DIM), F32)],
        compiler_params=_params(2),
        name="sb_attention",
    )(q, k, v, gate, uo)


def _gelu_tanh(x):
    c = math.sqrt(2.0 / math.pi)
    return 0.5 * x * (1.0 + jnp.tanh(c * (x + 0.044715 * (x * x * x))))


def _s5_kernel(u_ref, gb_ref, bre_ref, bim_ref, cre_ref, cimn_ref, lre_ref, lim_ref,
               d_ref, wglu_ref, bglu_ref, o_ref,
               us_ref, ut_ref, bure_ref, buim_ref, hre_ref, him_ref,
               hsre_ref, hsim_ref, ys_ref, yn_ref):
    bsz, tt, width = u_ref.shape
    rows = bsz * tt
    nstate = bure_ref.shape[1]
    nsub = tt // S5_SUB
    nslab = width // LANES

    @pl.when(pl.program_id(0) == 0)
    def _():
        hsre_ref[...] = jnp.zeros_like(hsre_ref)
        hsim_ref[...] = jnp.zeros_like(hsim_ref)

    for c in range(nslab):
        for b in range(bsz):
            for sub in range(nsub):
                r0 = sub * bsz * S5_SUB + b * S5_SUB
                us_ref[c, r0:r0 + S5_SUB, :] = u_ref[b, sub * S5_SUB:(sub + 1) * S5_SUB,
                                                     c * LANES:(c + 1) * LANES]
    for sub in range(nsub):
        for t8 in range(S5_SUB):
            step = sub * S5_SUB + t8
            for c in range(nslab):
                ut_ref[step * bsz:(step + 1) * bsz, c * LANES:(c + 1) * LANES] = (
                    us_ref[c, pl.ds(sub * bsz * S5_SUB + t8, bsz, stride=S5_SUB), :])

    ut = ut_ref[...].astype(BF16)
    groups_per_tile = MXU_DIM // S5_STATE
    in_per_tile = groups_per_tile * S5_GROUP
    for j in range(nstate // MXU_DIM):
        kc = (j * in_per_tile) // MXU_DIM
        lhs = ut[:, kc * MXU_DIM:(kc + 1) * MXU_DIM]
        ksl = slice(kc * MXU_DIM, (kc + 1) * MXU_DIM)
        nsl = slice(j * MXU_DIM, (j + 1) * MXU_DIM)
        bure_ref[:, nsl] = _dot(lhs, bre_ref[ksl, nsl])
        buim_ref[:, nsl] = _dot(lhs, bim_ref[ksl, nsl])

    chunk = 512
    for c in range(nstate // chunk):
        sl = slice(c * chunk, (c + 1) * chunk)
        lr = jnp.broadcast_to(lre_ref[:, sl], (bsz, chunk))
        li = jnp.broadcast_to(lim_ref[:, sl], (bsz, chunk))
        hr = hsre_ref[:, sl]
        hi = hsim_ref[:, sl]
        for step in range(tt):
            rsl = slice(step * bsz, (step + 1) * bsz)
            hr, hi = (lr * hr - li * hi + bure_ref[rsl, sl],
                      lr * hi + li * hr + buim_ref[rsl, sl])
            hre_ref[rsl, sl] = hr.astype(BF16)
            him_ref[rsl, sl] = hi.astype(BF16)
        hsre_ref[:, sl] = hr
        hsim_ref[:, sl] = hi

    tiles_per_slab = (LANES // S5_GROUP) * S5_STATE // MXU_DIM
    for m in range(nslab):
        acc = None
        nsl = slice(m * LANES, (m + 1) * LANES)
        for k in range(m * tiles_per_slab, (m + 1) * tiles_per_slab):
            ksl = slice(k * MXU_DIM, (k + 1) * MXU_DIM)
            d = _dot(hre_ref[:, ksl], cre_ref[ksl, nsl]) + _dot(him_ref[:, ksl], cimn_ref[ksl, nsl])
            acc = d if acc is None else acc + d
        ys_ref[m] = acc
    for b in range(bsz):
        for c in range(nslab):
            yn_ref[b * tt:(b + 1) * tt, c * LANES:(c + 1) * LANES] = (
                ys_ref[c, pl.ds(b, tt, stride=bsz), :])

    u_nat = u_ref[...].reshape(rows, width)
    y = _gelu_tanh(yn_ref[...] + d_ref[...] * u_nat)
    glu = _dot(y.astype(BF16), wglu_ref[...]) + bglu_ref[...]
    out = y * _sigmoid(glu) * gb_ref[...].reshape(rows, width)
    o_ref[...] = out.reshape(bsz, tt, width)


def _s5(u, gb, bre, bim, cre, cimn, lre, lim, dskip, wglu, bglu):
    bsz, seq, width = u.shape
    nstate = bre.shape[1]
    tt = S5_TIME_TILE
    rows = bsz * tt
    blk = pl.BlockSpec((bsz, tt, width), lambda i: (0, i, 0))
    return pl.pallas_call(
        _s5_kernel,
        grid=(seq // tt,),
        in_specs=[blk, blk, _full(bre.shape), _full(bim.shape), _full(cre.shape),
                  _full(cimn.shape), _full(lre.shape), _full(lim.shape), _full(dskip.shape),
                  _full(wglu.shape), _full(bglu.shape)],
        out_specs=blk,
        out_shape=jax.ShapeDtypeStruct((bsz, seq, width), F32),
        scratch_shapes=[
            pltpu.VMEM((width // LANES, rows, LANES), F32),
            pltpu.VMEM((rows, width), F32),
            pltpu.VMEM((rows, nstate), F32),
            pltpu.VMEM((rows, nstate), F32),
            pltpu.VMEM((rows, nstate), BF16),
            pltpu.VMEM((rows, nstate), BF16),
            pltpu.VMEM((bsz, nstate), F32),
            pltpu.VMEM((bsz, nstate), F32),
            pltpu.VMEM((width // LANES, rows, LANES), F32),
            pltpu.VMEM((rows, width), F32),
        ],
        compiler_params=_params(),
        name="s5_mixer",
    )(u, gb, bre, bim, cre, cimn, lre, lim, dskip, wglu, bglu)


def _s5_weights(lam_re, lam_im, log_dt, b_re, b_im, c_re, c_im):
    g, n = lam_re.shape
    p = b_re.shape[-1]
    lam = lax.complex(lam_re.astype(F32), lam_im.astype(F32))
    dt = jnp.exp(log_dt.astype(F32))[:, None]
    lam_bar = jnp.exp(lam * dt)
    b_bar = ((lam_bar - 1.0) / lam)[..., None] * lax.complex(b_re.astype(F32), b_im.astype(F32))
    eye = jnp.eye(g, dtype=BF16)
    bd_in = lambda m: jnp.einsum('gnp,gh->gphn', m.astype(BF16), eye).reshape(g * p, g * n)
    bd_out = lambda m: jnp.einsum('gpn,gh->gnhp', m.astype(BF16), eye).reshape(g * n, g * p)
    return (bd_in(b_bar.real), bd_in(b_bar.imag), bd_out(c_re), bd_out(-c_im),
            lam_bar.real.reshape(1, g * n), lam_bar.imag.reshape(1, g * n))


def _odd_in_body(xs, rows, g_ref, w_ref, wr_ref, wg_ref, bg_ref,
                 q_ref, k_ref, v_ref, gz_ref, la_ref):
    key = GLA_HEADS * GLA_DK
    val = GLA_HEADS * GLA_DV
    hs = [_rms(x, g_ref[...]).astype(BF16) for x in xs]
    parts = range(len(xs))
    seg = 512

    def proj(r, lo):
        return _dot(hs[r], w_ref[:, lo:lo + seg])

    codes = [_dot(hs[r], wr_ref[...]).astype(BF16) for r in parts]
    for r in parts:
        q_ref[rows[r], :] = proj(r, 0) * (GLA_DK ** -0.5)
    for r in parts:
        k_ref[rows[r], :] = proj(r, key)
    for r in parts:
        pre = _dot(codes[r], wg_ref[...]) + bg_ref[...]
        la_ref[rows[r], :] = -_softplus(-pre) * (1.0 / GLA_TAU)
    for s in range(val // seg):
        for r in parts:
            gz_ref[rows[r], s * seg:(s + 1) * seg] = _silu(proj(r, 2 * key + val + s * seg)).astype(BF16)
    for s in range(val // seg):
        for r in parts:
            v_ref[rows[r], s * seg:(s + 1) * seg] = proj(r, 2 * key + s * seg).astype(BF16)


def _gla_kernel(q_ref, k_ref, la_ref, v_ref, gz_ref, tri_ref, og_ref, o_ref,
                st_ref, qd_ref, sc_ref, ut_ref):
    c = GLA_CHUNK
    nchunk = q_ref.shape[1] // c
    items = [(i, hd) for i in range(nchunk) for hd in range(GLA_HEADS)]

    @pl.when(pl.program_id(1) == 0)
    def _():
        st_ref[...] = jnp.zeros_like(st_ref)

    tri2 = tri_ref[...]
    row = lax.broadcasted_iota(jnp.int32, (c, c), 0)
    col = lax.broadcasted_iota(jnp.int32, (c, c), 1)
    causal = col <= row
    og = og_ref[...]
    rows = lambda i: slice(i * c, (i + 1) * c)
    ksl = lambda hd: slice(hd * GLA_DK, (hd + 1) * GLA_DK)
    vsl = lambda hd: slice(hd * GLA_DV, (hd + 1) * GLA_DV)

    gs = []
    for i, hd in items:
        hi, lo = _split_bf16(la_ref[0, rows(i), ksl(hd)])
        gs.append(_dot(tri2, jnp.concatenate([hi, lo], axis=0)))
    k_invs, k_decs, e_lasts = [], [], []
    for n, (i, hd) in enumerate(items):
        g = gs[n]
        g_last = g[c - 1:c, :]
        k = k_ref[0, rows(i), ksl(hd)]
        qd_ref[i, hd] = (q_ref[0, rows(i), ksl(hd)] * jnp.exp(g)).astype(BF16)
        k_invs.append((k * jnp.exp(-g)).astype(BF16))
        k_decs.append((k * jnp.exp(g_last - g)).astype(BF16))
        e_lasts.append(jnp.exp(g_last))
    for n, (i, hd) in enumerate(items):
        scores = jnp.where(causal, _dot_nt(qd_ref[i, hd], k_invs[n]), 0.0)
        sc_ref[i, hd] = scores.astype(BF16)
    for n, (i, hd) in enumerate(items):
        ut_ref[i, hd] = lax.dot_general(v_ref[0, rows(i), vsl(hd)], k_decs[n],
                                        (((0,), (0,)), ((), ())), preferred_element_type=F32)

    for n, (i, hd) in enumerate(items):
        st = st_ref[hd]
        o = _dot(sc_ref[i, hd], v_ref[0, rows(i), vsl(hd)]) + _dot_nt(qd_ref[i, hd], st.astype(BF16))
        st_ref[hd] = st * e_lasts[n] + ut_ref[i, hd]
        o = _rms(o, og) * gz_ref[0, rows(i), vsl(hd)].astype(F32)
        o_ref[0, rows(i), vsl(hd)] = o.astype(BF16)


def _gla(q, k, la, v, gz, tri2, og):
    bsz, seq, key = q.shape
    val = v.shape[2]
    tt = GLA_TIME_TILE
    nchunk = tt // GLA_CHUNK
    kspec = pl.BlockSpec((1, tt, key), lambda b, i: (b, i, 0))
    vspec = pl.BlockSpec((1, tt, val), lambda b, i: (b, i, 0))
    return pl.pallas_call(
        _gla_kernel,
        grid=(bsz, seq // tt),
        in_specs=[kspec, kspec, kspec, vspec, vspec, _full(tri2.shape), _full(og.shape)],
        out_specs=vspec,
        out_shape=jax.ShapeDtypeStruct((bsz, seq, val), BF16),
        scratch_shapes=[
            pltpu.VMEM((GLA_HEADS, GLA_DV, GLA_DK), F32),
            pltpu.VMEM((nchunk, GLA_HEADS, GLA_CHUNK, GLA_DK), BF16),
            pltpu.VMEM((nchunk, GLA_HEADS, GLA_CHUNK, GLA_CHUNK), BF16),
            pltpu.VMEM((nchunk, GLA_HEADS, GLA_DV, GLA_DK), F32),
        ],
        compiler_params=_params(2),
        name="gla",
    )(q, k, la, v, gz, tri2, og)


def kernel(x, even_norm_g, even_w_in, sb_q_norm_g, sb_k_norm_g, s5_lambda_re, s5_lambda_im,
           s5_log_dt, s5_b_re, s5_b_im, s5_c_re, s5_c_im, s5_d, s5_w_glu, s5_b_glu, even_w_out,
           odd_norm_g, odd_w_in, gla_w_gate, gla_b_gate, gla_o_norm_g, odd_w_out):
    bsz, seq, d = x.shape
    n = bsz * seq
    depth = even_norm_g.shape[0] + odd_norm_g.shape[0]
    sb_width = SB_HEADS * SB_HEAD_DIM
    key = GLA_HEADS * GLA_DK
    val = GLA_HEADS * GLA_DV
    assert n % ROW_TILE == 0 and seq % (SB_QGROUPS * SB_BLOCK) == 0 and seq % GLA_TIME_TILE == 0
    assert seq % S5_TIME_TILE == 0 and bsz % 8 == 0

    t = SB_BLOCK
    j_idx = jnp.arange(t)[:, None]
    s_idx = jnp.arange(t)[None, :]
    uo = jnp.concatenate([(j_idx >= s_idx), jnp.ones((t, t), bool)], axis=1).astype(BF16)
    uo = jnp.concatenate([uo, uo], axis=0)
    c_idx = jnp.arange(GLA_CHUNK)
    tri = (c_idx[None, :] <= c_idx[:, None]).astype(BF16)
    tri = jnp.concatenate([tri, tri], axis=1)

    def in_side(layer):
        i = layer // 2
        if layer >= depth:
            return None, []
        if layer % 2 == 0:
            return "even", [even_norm_g[i].reshape(1, d).astype(F32), even_w_in_bf[i],
                            sb_q_norm_g[i].reshape(1, -1).astype(F32),
                            sb_k_norm_g[i].reshape(1, -1).astype(F32)]
        return "odd", [odd_norm_g[i].reshape(1, d).astype(F32), odd_w_main[i], odd_w_code[i],
                       gate_w[i], gla_b_gate[i].reshape(1, key).astype(F32)]

    main = 2 * key + 2 * val
    even_w_in_bf = even_w_in.astype(BF16)
    even_w_out_bf = even_w_out.astype(BF16)
    odd_w_main = odd_w_in[:, :, :main].astype(BF16)
    odd_w_code = jnp.pad(odd_w_in[:, :, main:], ((0, 0), (0, 0), (0, LANES - GLA_RANK))).astype(BF16)
    gate_w = jnp.pad(gla_w_gate, ((0, 0), (0, LANES - GLA_RANK), (0, 0))).astype(BF16)
    odd_w_out_bf = odd_w_out.astype(BF16)
    glu_w = s5_w_glu.astype(BF16)
    s5_all = jax.vmap(_s5_weights)(s5_lambda_re, s5_lambda_im, s5_log_dt, s5_b_re, s5_b_im,
                                   s5_c_re, s5_c_im)

    x2 = x.reshape(n, d).astype(F32)
    x2, proj = _layer_io(x2, [], None, *in_side(0))
    for layer in range(depth):
        i = layer // 2
        if layer % 2 == 0:
            q, k, v, ga, u, gb = proj
            shp = (bsz, seq, sb_width)
            o_a = _sb_attn(q.reshape(shp), k.reshape(shp), v.reshape(shp), ga.reshape(shp), uo)
            s5w = u.shape[1]
            bre, bim, cre, cimn, lre, lim = [a[i] for a in s5_all]
            o_b = _s5(u.reshape(bsz, seq, s5w), gb.reshape(bsz, seq, s5w), bre, bim, cre, cimn,
                      lre, lim, s5_d[i].reshape(1, s5w).astype(F32), glu_w[i],
                      s5_b_glu[i].reshape(1, s5w).astype(F32))
            acts, w_out = [o_a.reshape(n, sb_width), o_b.reshape(n, s5w)], even_w_out_bf[i]
        else:
            q, k, v, gz, la = proj
            o = _gla(q.reshape(bsz, seq, key), k.reshape(bsz, seq, key), la.reshape(bsz, seq, key),
                     v.reshape(bsz, seq, val), gz.reshape(bsz, seq, val), tri,
                     gla_o_norm_g[i].reshape(1, GLA_DV).astype(F32))
            acts, w_out = [o.reshape(n, val)], odd_w_out_bf[i]
        x2, proj = _layer_io(x2, acts, w_out, *in_side(layer + 1))
    return x2.reshape(bsz, seq, d).astype(x.dtype)
```

```python
import functools
import math

import jax
import jax.numpy as jnp
from jax import lax
from jax.experimental import pallas as pl
from jax.experimental.pallas import tpu as pltpu

F32 = jnp.float32
BF16 = jnp.bfloat16

EPS = 1e-6
SB_HEADS = 8
SB_HEAD_DIM = 128
S5_GROUP = 16
S5_STATE = 64
GLA_HEADS = 4
GLA_DK = 128
GLA_DV = 256
GLA_RANK = 16
GLA_TAU = 16.0
GLA_CHUNK = 64

LANES = 128
MXU_DIM = 256
VMEM_LIMIT = 56 * 1024 * 1024

ROW_TILE = 512
ROW_PARTS = 2
SB_BLOCK = 128
SB_QGROUPS = 2
SB_DEAD_LOG2 = 127.0
LOG2E = 1.4426950408889634
S5_TIME_TILE = 32
S5_SUB = 8
GLA_TIME_TILE = 1024


def _dot(a, b):
    return jnp.dot(a, b, preferred_element_type=F32)


def _dot_nt(a, b):
    return lax.dot_general(a, b, (((1,), (1,)), ((), ())), preferred_element_type=F32)


def _rms(x, g):
    return x * lax.rsqrt(jnp.mean(x * x, axis=-1, keepdims=True) + EPS) * g


def _sigmoid(x):
    return 1.0 / (1.0 + jnp.exp(-x))


def _silu(x):
    return x * _sigmoid(x)


def _softplus(x):
    neg_abs = pltpu.bitcast(pltpu.bitcast(x, jnp.uint32) | jnp.uint32(0x80000000), F32)
    return jnp.maximum(x, 0.0) + jnp.log(1.0 + jnp.exp(neg_abs))


def _softplus2(x):
    neg_abs = pltpu.bitcast(pltpu.bitcast(x, jnp.uint32) | jnp.uint32(0x80000000), F32)
    return jnp.maximum(x, 0.0) + jnp.log(1.0 + jnp.exp2(neg_abs)) * LOG2E


def _split_bf16(x):
    hi = x.astype(BF16)
    lo = (x - hi.astype(F32)).astype(BF16)
    return hi, lo


def _params(n_axes=1):
    return pltpu.CompilerParams(dimension_semantics=("arbitrary",) * n_axes,
                                vmem_limit_bytes=VMEM_LIMIT)


def _full(shape):
    return pl.BlockSpec(shape, lambda *_: (0,) * len(shape))


def _even_in_body(xs, rows, g_ref, w_ref, qg_ref, kg_ref,
                  q_ref, k_ref, v_ref, ga_ref, u_ref, gb_ref):
    width = SB_HEADS * SB_HEAD_DIM
    hs = [_rms(x, g_ref[...]).astype(BF16) for x in xs]
    parts = range(len(xs))
    seg = 512

    def proj(r, lo):
        return _dot(hs[r], w_ref[:, lo:lo + seg])

    for ref, gain_ref, off, scale in ((q_ref, qg_ref, 0, SB_HEAD_DIM ** -0.5 * LOG2E),
                                      (k_ref, kg_ref, width, 1.0)):
        gain = gain_ref[...] * scale
        for s in range(width // seg):
            for r in parts:
                p = proj(r, off + s * seg)
                for hd in range(seg // SB_HEAD_DIM):
                    ph = p[:, hd * SB_HEAD_DIM:(hd + 1) * SB_HEAD_DIM]
                    lo = s * seg + hd * SB_HEAD_DIM
                    ref[rows[r], lo:lo + SB_HEAD_DIM] = _rms(ph, gain).astype(BF16)
    for s in range(width // seg):
        for r in parts:
            v_ref[rows[r], s * seg:(s + 1) * seg] = proj(r, 2 * width + s * seg).astype(BF16)
    for s in range(width // seg):
        for r in parts:
            ga_ref[rows[r], s * seg:(s + 1) * seg] = _silu(proj(r, 3 * width + s * seg)).astype(BF16)
    for r in parts:
        gb_ref[rows[r], :] = _silu(proj(r, 4 * width + seg))
    for r in parts:
        u_ref[rows[r], :] = proj(r, 4 * width)


def _io_kernel(*refs, n_acts, in_kind):
    x_ref, refs = refs[0], refs[1:]
    tm = x_ref.shape[0]
    rows = [slice(r * tm // ROW_PARTS, (r + 1) * tm // ROW_PARTS) for r in range(ROW_PARTS)]
    xs = [x_ref[rw, :] for rw in rows]
    n_in = {None: 0, "even": 4, "odd": 5}[in_kind]
    if n_acts:
        act_refs, wout_ref, refs = refs[:n_acts], refs[n_acts], refs[n_acts + 1:]
        for r, rw in enumerate(rows):
            off = 0
            for a_ref in act_refs:
                kdim = a_ref.shape[1]
                xs[r] = xs[r] + _dot(a_ref[rw, :].astype(BF16), wout_ref[off:off + kdim, :])
                off += kdim
            refs[n_in][rw, :] = xs[r]
        in_refs, out_refs = refs[:n_in], refs[n_in + 1:]
    else:
        in_refs, out_refs = refs[:n_in], refs[n_in:]
    if in_kind == "even":
        _even_in_body(xs, rows, *in_refs, *out_refs)
    elif in_kind == "odd":
        _odd_in_body(xs, rows, *in_refs, *out_refs)


def _layer_io(x2, acts, w_out, in_kind, in_params):
    n, d = x2.shape
    row = lambda c: pl.BlockSpec((ROW_TILE, c), lambda i: (i, 0))
    once = lambda a: pl.BlockSpec(a.shape, lambda i: (0,) * a.ndim, pipeline_mode=pl.Buffered(1))
    sds = jax.ShapeDtypeStruct
    args, in_specs, out_specs, out_shape = [x2], [row(d)], [], []
    if acts:
        args += list(acts) + [w_out]
        in_specs += [row(a.shape[1]) for a in acts] + [once(w_out)]
        out_specs.append(row(d))
        out_shape.append(sds((n, d), F32))
    args += list(in_params)
    in_specs += [once(p) for p in in_params]
    if in_kind == "even":
        width = SB_HEADS * SB_HEAD_DIM
        s5w = (in_params[1].shape[1] - 4 * width) // 2
        outs = [(width, BF16)] * 4 + [(s5w, F32)] * 2
    elif in_kind == "odd":
        key, val = GLA_HEADS * GLA_DK, GLA_HEADS * GLA_DV
        outs = [(key, F32), (key, F32), (val, BF16), (val, BF16), (key, F32)]
    else:
        outs = []
    out_specs += [row(c) for c, _ in outs]
    out_shape += [sds((n, c), dt) for c, dt in outs]
    res = pl.pallas_call(
        functools.partial(_io_kernel, n_acts=len(acts), in_kind=in_kind),
        grid=(n // ROW_TILE,),
        in_specs=in_specs,
        out_specs=out_specs,
        out_shape=out_shape,
        compiler_params=_params(),
        name=("out_" if acts else "") + (in_kind + "_in_" if in_kind else "") + "proj",
    )(*args)
    return (res[0], res[1:]) if acts else (x2, res)


def _sb_attn_kernel(q_ref, k_ref, v_ref, gate_ref, uo_ref, o_ref, carry_ref, acc_ref):
    t = SB_BLOCK
    ng = SB_QGROUPS
    heads = q_ref.shape[2] // SB_HEAD_DIM
    base = pl.program_id(1) * ng
    uo = uo_ref[...]
    row = lax.broadcasted_iota(jnp.int32, (t, t), 0)
    col = lax.broadcasted_iota(jnp.int32, (t, t), 1)
    strict = col < row
    lanes = [slice(hd * SB_HEAD_DIM, (hd + 1) * SB_HEAD_DIM) for hd in range(heads)]
    qrows = [slice(g * t, (g + 1) * t) for g in range(ng)]

    def sweep(blocks):
        krows = {(g, n): pl.ds(pl.multiple_of(kb * t, t), t)
                 for g, blk in blocks.items() for n, (kb, _) in enumerate(blk)}
        items = [(g, n, hd) for (g, n) in krows for hd in range(heads)]
        zs = {(g, n, hd): _dot_nt(q_ref[0, qrows[g], lanes[hd]], k_ref[0, krows[g, n], lanes[hd]])
              for g, n, hd in items}
        r2s = {}
        for g, n, hd in items:
            sp = _softplus2(zs[g, n, hd])
            hi, lo = _split_bf16(jnp.where(strict, sp, 0.0) if blocks[g][n][1] else sp)
            r2s[g, n, hd] = _dot(jnp.concatenate([hi, lo], axis=1), uo)
        low = None
        for g, blk in blocks.items():
            fresh = blk[0][1]
            for hd in range(heads):
                carry = None if fresh else carry_ref[g, hd]
                pv = None
                for n, (_, diag) in enumerate(blk):
                    r2 = r2s[g, n, hd]
                    rest = r2[:, :t] if carry is None else r2[:, :t] + carry
                    carry = r2[:, t:] if carry is None else r2[:, t:] + carry
                    w = jnp.exp2(zs[g, n, hd] - rest)
                    if diag:
                        w = jnp.where(strict, w, 0.0)
                    d = _dot(w.astype(BF16), v_ref[0, krows[g, n], lanes[hd]])
                    pv = d if pv is None else pv + d
                carry_ref[g, hd] = carry
                low = carry if low is None else jnp.minimum(low, carry)
                acc_ref[g, hd] = pv if fresh else acc_ref[g, hd] + pv
        return jnp.min(low) < SB_DEAD_LOG2

    def first_step():
        sweep({g: [(g - n, n == 0) for n in range(g + 1)] for g in range(ng)})
        return jnp.bool_(False)

    def later_step():
        return sweep({g: [(base + g, True), (base + g - 1, False)] for g in range(ng)})

    def cond(state):
        j, live = state
        return jnp.logical_and(j <= base, live)

    def body(state):
        j, _ = state
        return j + 1, sweep({g: [(base + g - j, False)] for g in range(ng)})

    _, live = lax.while_loop(cond, body, (jnp.int32(2), lax.cond(base == 0, first_step, later_step)))

    for extra in range(1, ng):
        @pl.when(live)
        def _():
            sweep({g: [(g - extra, False)] for g in range(extra, ng)})

    for g in range(ng):
        for hd in range(heads):
            o_ref[0, qrows[g], lanes[hd]] = (
                acc_ref[g, hd] * gate_ref[0, qrows[g], lanes[hd]].astype(F32)).astype(BF16)


def _sb_attn(q, k, v, gate, uo):
    bsz, seq, width = q.shape
    t = SB_BLOCK
    tq = SB_QGROUPS * t
    heads = width // SB_HEAD_DIM
    qspec = pl.BlockSpec((1, tq, width), lambda b, i: (b, i, 0))
    kspec = pl.BlockSpec((1, seq, width), lambda b, i: (b, 0, 0))
    return pl.pallas_call(
        _sb_attn_kernel,
        grid=(bsz, seq // tq),
        in_specs=[qspec, kspec, kspec, qspec, _full(uo.shape)],
        out_specs=qspec,
        out_shape=jax.ShapeDtypeStruct((bsz, seq, width), BF16),
        scratch_shapes=[pltpu.VMEM((SB_QGROUPS, heads, t, t), F32),
                        pltpu.VMEM((SB_QGROUPS, heads, t, SB_HEAD_DIM), F32)],
        compiler_params=_params(2),
        name="sb_attention",
    )(q, k, v, gate, uo)


def _gelu_tanh(x):
    c = math.sqrt(2.0 / math.pi)
    return 0.5 * x * (1.0 + jnp.tanh(c * (x + 0.044715 * (x * x * x))))


def _s5_kernel(u_ref, gb_ref, bre_ref, bim_ref, cre_ref, cimn_ref, lre_ref, lim_ref,
               d_ref, wglu_ref, bglu_ref, o_ref,
               us_ref, ut_ref, bure_ref, buim_ref, hre_ref, him_ref,
               hsre_ref, hsim_ref, ys_ref, yn_ref):
    bsz, tt, width = u_ref.shape
    rows = bsz * tt
    nstate = bure_ref.shape[1]
    nsub = tt // S5_SUB
    nslab = width // LANES

    @pl.when(pl.program_id(0) == 0)
    def _():
        hsre_ref[...] = jnp.zeros_like(hsre_ref)
        hsim_ref[...] = jnp.zeros_like(hsim_ref)

    for c in range(nslab):
        for b in range(bsz):
            for sub in range(nsub):
                r0 = sub * bsz * S5_SUB + b * S5_SUB
                us_ref[c, r0:r0 + S5_SUB, :] = u_ref[b, sub * S5_SUB:(sub + 1) * S5_SUB,
                                                     c * LANES:(c + 1) * LANES]
    for sub in range(nsub):
        for t8 in range(S5_SUB):
            step = sub * S5_SUB + t8
            for c in range(nslab):
                ut_ref[step * bsz:(step + 1) * bsz, c * LANES:(c + 1) * LANES] = (
                    us_ref[c, pl.ds(sub * bsz * S5_SUB + t8, bsz, stride=S5_SUB), :])

    ut = ut_ref[...].astype(BF16)
    groups_per_tile = MXU_DIM // S5_STATE
    in_per_tile = groups_per_tile * S5_GROUP
    for j in range(nstate // MXU_DIM):
        kc = (j * in_per_tile) // MXU_DIM
        lhs = ut[:, kc * MXU_DIM:(kc + 1) * MXU_DIM]
        ksl = slice(kc * MXU_DIM, (kc + 1) * MXU_DIM)
        nsl = slice(j * MXU_DIM, (j + 1) * MXU_DIM)
        bure_ref[:, nsl] = _dot(lhs, bre_ref[ksl, nsl])
        buim_ref[:, nsl] = _dot(lhs, bim_ref[ksl, nsl])

    chunk = 512
    for c in range(nstate // chunk):
        sl = slice(c * chunk, (c + 1) * chunk)
        lr = jnp.broadcast_to(lre_ref[:, sl], (bsz, chunk))
        li = jnp.broadcast_to(lim_ref[:, sl], (bsz, chunk))
        hr = hsre_ref[:, sl]
        hi = hsim_ref[:, sl]
        for step in range(tt):
            rsl = slice(step * bsz, (step + 1) * bsz)
            hr, hi = (lr * hr - li * hi + bure_ref[rsl, sl],
                      lr * hi + li * hr + buim_ref[rsl, sl])
            hre_ref[rsl, sl] = hr.astype(BF16)
            him_ref[rsl, sl] = hi.astype(BF16)
        hsre_ref[:, sl] = hr
        hsim_ref[:, sl] = hi

    tiles_per_slab = (LANES // S5_GROUP) * S5_STATE // MXU_DIM
    for m in range(nslab):
        acc = None
        nsl = slice(m * LANES, (m + 1) * LANES)
        for k in range(m * tiles_per_slab, (m + 1) * tiles_per_slab):
            ksl = slice(k * MXU_DIM, (k + 1) * MXU_DIM)
            d = _dot(hre_ref[:, ksl], cre_ref[ksl, nsl]) + _dot(him_ref[:, ksl], cimn_ref[ksl, nsl])
            acc = d if acc is None else acc + d
        ys_ref[m] = acc
    for b in range(bsz):
        for c in range(nslab):
            yn_ref[b * tt:(b + 1) * tt, c * LANES:(c + 1) * LANES] = (
                ys_ref[c, pl.ds(b, tt, stride=bsz), :])

    u_nat = u_ref[...].reshape(rows, width)
    y = _gelu_tanh(yn_ref[...] + d_ref[...] * u_nat)
    glu = _dot(y.astype(BF16), wglu_ref[...]) + bglu_ref[...]
    out = y * _sigmoid(glu) * gb_ref[...].reshape(rows, width)
    o_ref[...] = out.reshape(bsz, tt, width)


def _s5(u, gb, bre, bim, cre, cimn, lre, lim, dskip, wglu, bglu):
    bsz, seq, width = u.shape
    nstate = bre.shape[1]
    tt = S5_TIME_TILE
    rows = bsz * tt
    blk = pl.BlockSpec((bsz, tt, width), lambda i: (0, i, 0))
    return pl.pallas_call(
        _s5_kernel,
        grid=(seq // tt,),
        in_specs=[blk, blk, _full(bre.shape), _full(bim.shape), _full(cre.shape),
                  _full(cimn.shape), _full(lre.shape), _full(lim.shape), _full(dskip.shape),
                  _full(wglu.shape), _full(bglu.shape)],
        out_specs=blk,
        out_shape=jax.ShapeDtypeStruct((bsz, seq, width), F32),
        scratch_shapes=[
            pltpu.VMEM((width // LANES, rows, LANES), F32),
            pltpu.VMEM((rows, width), F32),
            pltpu.VMEM((rows, nstate), F32),
            pltpu.VMEM((rows, nstate), F32),
            pltpu.VMEM((rows, nstate), BF16),
            pltpu.VMEM((rows, nstate), BF16),
            pltpu.VMEM((bsz, nstate), F32),
            pltpu.VMEM((bsz, nstate), F32),
            pltpu.VMEM((width // LANES, rows, LANES), F32),
            pltpu.VMEM((rows, width), F32),
        ],
        compiler_params=_params(),
        name="s5_mixer",
    )(u, gb, bre, bim, cre, cimn, lre, lim, dskip, wglu, bglu)


def _s5_weights(lam_re, lam_im, log_dt, b_re, b_im, c_re, c_im):
    g, n = lam_re.shape
    lam = lax.complex(lam_re.astype(F32), lam_im.astype(F32))
    dt = jnp.exp(log_dt.astype(F32))[:, None]
    lam_bar = jnp.exp(lam * dt)
    b_bar = ((lam_bar - 1.0) / lam)[..., None] * lax.complex(b_re.astype(F32), b_im.astype(F32))

    def block_diag(m):
        cols, rows = m.shape[1], m.shape[2]
        stacked = jnp.transpose(m.astype(BF16), (0, 2, 1)).reshape(g * rows, cols)
        r_grp = lax.broadcasted_iota(jnp.int32, (g * rows, g * cols), 0) // rows
        c_grp = lax.broadcasted_iota(jnp.int32, (g * rows, g * cols), 1) // cols
        return jnp.where(r_grp == c_grp, jnp.tile(stacked, (1, g)), jnp.zeros((), BF16))

    return (block_diag(b_bar.real), block_diag(b_bar.imag),
            block_diag(c_re), block_diag(-c_im),
            lam_bar.real.reshape(1, g * n), lam_bar.imag.reshape(1, g * n))


def _odd_in_body(xs, rows, g_ref, w_ref, wr_ref, wg_ref, bg_ref,
                 q_ref, k_ref, v_ref, gz_ref, la_ref):
    key = GLA_HEADS * GLA_DK
    val = GLA_HEADS * GLA_DV
    hs = [_rms(x, g_ref[...]).astype(BF16) for x in xs]
    parts = range(len(xs))
    seg = 512

    def proj(r, lo):
        return _dot(hs[r], w_ref[:, lo:lo + seg])

    codes = [_dot(hs[r], wr_ref[...]).astype(BF16) for r in parts]
    for r in parts:
        q_ref[rows[r], :] = proj(r, 0) * (GLA_DK ** -0.5)
    for r in parts:
        k_ref[rows[r], :] = proj(r, key)
    for r in parts:
        pre = _dot(codes[r], wg_ref[...]) + bg_ref[...]
        la_ref[rows[r], :] = -_softplus(-pre) * (1.0 / GLA_TAU)
    for s in range(val // seg):
        for r in parts:
            gz_ref[rows[r], s * seg:(s + 1) * seg] = _silu(proj(r, 2 * key + val + s * seg)).astype(BF16)
    for s in range(val // seg):
        for r in parts:
            v_ref[rows[r], s * seg:(s + 1) * seg] = proj(r, 2 * key + s * seg).astype(BF16)


def _gla_kernel(q_ref, k_ref, la_ref, v_ref, gz_ref, tri_ref, og_ref, o_ref,
                st_ref, qd_ref, sc_ref, ut_ref):
    c = GLA_CHUNK
    nchunk = q_ref.shape[1] // c
    items = [(i, hd) for i in range(nchunk) for hd in range(GLA_HEADS)]

    @pl.when(pl.program_id(1) == 0)
    def _():
        st_ref[...] = jnp.zeros_like(st_ref)

    tri2 = tri_ref[...]
    row = lax.broadcasted_iota(jnp.int32, (c, c), 0)
    col = lax.broadcasted_iota(jnp.int32, (c, c), 1)
    causal = col <= row
    og = og_ref[...]
    rows = lambda i: slice(i * c, (i + 1) * c)
    ksl = lambda hd: slice(hd * GLA_DK, (hd + 1) * GLA_DK)
    vsl = lambda hd: slice(hd * GLA_DV, (hd + 1) * GLA_DV)

    gs = []
    for i, hd in items:
        hi, lo = _split_bf16(la_ref[0, rows(i), ksl(hd)])
        gs.append(_dot(tri2, jnp.concatenate([hi, lo], axis=0)))
    k_invs, k_decs, e_lasts = [], [], []
    for n, (i, hd) in enumerate(items):
        g = gs[n]
        g_last = g[c - 1:c, :]
        k = k_ref[0, rows(i), ksl(hd)]
        qd_ref[i, hd] = (q_ref[0, rows(i), ksl(hd)] * jnp.exp(g)).astype(BF16)
        k_invs.append((k * jnp.exp(-g)).astype(BF16))
        k_decs.append((k * jnp.exp(g_last - g)).astype(BF16))
        e_lasts.append(jnp.exp(g_last))
    for n, (i, hd) in enumerate(items):
        scores = jnp.where(causal, _dot_nt(qd_ref[i, hd], k_invs[n]), 0.0)
        sc_ref[i, hd] = scores.astype(BF16)
    for n, (i, hd) in enumerate(items):
        ut_ref[i, hd] = lax.dot_general(v_ref[0, rows(i), vsl(hd)], k_decs[n],
                                        (((0,), (0,)), ((), ())), preferred_element_type=F32)

    for n, (i, hd) in enumerate(items):
        st = st_ref[hd]
        o = _dot(sc_ref[i, hd], v_ref[0, rows(i), vsl(hd)]) + _dot_nt(qd_ref[i, hd], st.astype(BF16))
        st_ref[hd] = st * e_lasts[n] + ut_ref[i, hd]
        o = _rms(o, og) * gz_ref[0, rows(i), vsl(hd)].astype(F32)
        o_ref[0, rows(i), vsl(hd)] = o.astype(BF16)


def _gla(q, k, la, v, gz, tri2, og):
    bsz, seq, key = q.shape
    val = v.shape[2]
    tt = GLA_TIME_TILE
    nchunk = tt // GLA_CHUNK
    kspec = pl.BlockSpec((1, tt, key), lambda b, i: (b, i, 0))
    vspec = pl.BlockSpec((1, tt, val), lambda b, i: (b, i, 0))
    return pl.pallas_call(
        _gla_kernel,
        grid=(bsz, seq // tt),
        in_specs=[kspec, kspec, kspec, vspec, vspec, _full(tri2.shape), _full(og.shape)],
        out_specs=vspec,
        out_shape=jax.ShapeDtypeStruct((bsz, seq, val), BF16),
        scratch_shapes=[
            pltpu.VMEM((GLA_HEADS, GLA_DV, GLA_DK), F32),
            pltpu.VMEM((nchunk, GLA_HEADS, GLA_CHUNK, GLA_DK), BF16),
            pltpu.VMEM((nchunk, GLA_HEADS, GLA_CHUNK, GLA_CHUNK), BF16),
            pltpu.VMEM((nchunk, GLA_HEADS, GLA_DV, GLA_DK), F32),
        ],
        compiler_params=_params(2),
        name="gla",
    )(q, k, la, v, gz, tri2, og)


def kernel(x, even_norm_g, even_w_in, sb_q_norm_g, sb_k_norm_g, s5_lambda_re, s5_lambda_im,
           s5_log_dt, s5_b_re, s5_b_im, s5_c_re, s5_c_im, s5_d, s5_w_glu, s5_b_glu, even_w_out,
           odd_norm_g, odd_w_in, gla_w_gate, gla_b_gate, gla_o_norm_g, odd_w_out):
    bsz, seq, d = x.shape
    n = bsz * seq
    depth = even_norm_g.shape[0] + odd_norm_g.shape[0]
    sb_width = SB_HEADS * SB_HEAD_DIM
    key = GLA_HEADS * GLA_DK
    val = GLA_HEADS * GLA_DV
    assert n % ROW_TILE == 0 and seq % (SB_QGROUPS * SB_BLOCK) == 0 and seq % GLA_TIME_TILE == 0
    assert seq % S5_TIME_TILE == 0 and bsz % 8 == 0

    t = SB_BLOCK
    j_idx = jnp.arange(t)[:, None]
    s_idx = jnp.arange(t)[None, :]
    uo = jnp.concatenate([(j_idx >= s_idx), jnp.ones((t, t), bool)], axis=1).astype(BF16)
    uo = jnp.concatenate([uo, uo], axis=0)
    c_idx = jnp.arange(GLA_CHUNK)
    tri = (c_idx[None, :] <= c_idx[:, None]).astype(BF16)
    tri = jnp.concatenate([tri, tri], axis=1)

    def in_side(layer):
        i = layer // 2
        if layer >= depth:
            return None, []
        if layer % 2 == 0:
            return "even", [even_norm_g[i].reshape(1, d).astype(F32), even_w_in_bf[i],
                            sb_q_norm_g[i].reshape(1, -1).astype(F32),
                            sb_k_norm_g[i].reshape(1, -1).astype(F32)]
        return "odd", [odd_norm_g[i].reshape(1, d).astype(F32), odd_w_main[i], odd_w_code[i],
                       gate_w[i], gla_b_gate[i].reshape(1, key).astype(F32)]

    main = 2 * key + 2 * val
    even_w_in_bf = even_w_in.astype(BF16)
    even_w_out_bf = even_w_out.astype(BF16)
    odd_w_main = odd_w_in[:, :, :main].astype(BF16)
    odd_w_code = jnp.pad(odd_w_in[:, :, main:], ((0, 0), (0, 0), (0, LANES - GLA_RANK))).astype(BF16)
    gate_w = jnp.pad(gla_w_gate, ((0, 0), (0, LANES - GLA_RANK), (0, 0))).astype(BF16)
    odd_w_out_bf = odd_w_out.astype(BF16)
    glu_w = s5_w_glu.astype(BF16)
    s5_all = jax.vmap(_s5_weights)(s5_lambda_re, s5_lambda_im, s5_log_dt, s5_b_re, s5_b_im,
                                   s5_c_re, s5_c_im)

    x2 = x.reshape(n, d).astype(F32)
    x2, proj = _layer_io(x2, [], None, *in_side(0))
    for layer in range(depth):
        i = layer // 2
        if layer % 2 == 0:
            q, k, v, ga, u, gb = proj
            shp = (bsz, seq, sb_width)
            o_a = _sb_attn(q.reshape(shp), k.reshape(shp), v.reshape(shp), ga.reshape(shp), uo)
            s5w = u.shape[1]
            bre, bim, cre, cimn, lre, lim = [a[i] for a in s5_all]
            o_b = _s5(u.reshape(bsz, seq, s5w), gb.reshape(bsz, seq, s5w), bre, bim, cre, cimn,
                      lre, lim, s5_d[i].reshape(1, s5w).astype(F32), glu_w[i],
                      s5_b_glu[i].reshape(1, s5w).astype(F32))
            acts, w_out = [o_a.reshape(n, sb_width), o_b.reshape(n, s5w)], even_w_out_bf[i]
        else:
            q, k, v, gz, la = proj
            o = _gla(q.reshape(bsz, seq, key), k.reshape(bsz, seq, key), la.reshape(bsz, seq, key),
                     v.reshape(bsz, seq, val), gz.reshape(bsz, seq, val), tri,
                     gla_o_norm_g[i].reshape(1, GLA_DV).astype(F32))
            acts, w_out = [o.reshape(n, val)], odd_w_out_bf[i]
        x2, proj = _layer_io(x2, acts, w_out, *in_side(layer + 1))
    return x2.reshape(bsz, seq, d).astype(x.dtype)
```

```python
import functools
import math

import jax
import jax.numpy as jnp
from jax import lax
from jax.experimental import pallas as pl
from jax.experimental.pallas import tpu as pltpu

F32 = jnp.float32
BF16 = jnp.bfloat16

EPS = 1e-6
SB_HEADS = 8
SB_HEAD_DIM = 128
S5_GROUP = 16
S5_STATE = 64
GLA_HEADS = 4
GLA_DK = 128
GLA_DV = 256
GLA_RANK = 16
GLA_TAU = 16.0
GLA_CHUNK = 64

LANES = 128
MXU_DIM = 256
VMEM_LIMIT = 56 * 1024 * 1024

ROW_TILE = 512
ROW_PARTS = 2
SB_BLOCK = 128
SB_QGROUPS = 2
SB_DEAD_LOG2 = 127.0
LOG2E = 1.4426950408889634
S5_TIME_TILE = 32
S5_SUB = 8
GLA_TIME_TILE = 1024


def _dot(a, b):
    return jnp.dot(a, b, preferred_element_type=F32)


def _dot_nt(a, b):
    return lax.dot_general(a, b, (((1,), (1,)), ((), ())), preferred_element_type=F32)


def _rms(x, g):
    return x * lax.rsqrt(jnp.mean(x * x, axis=-1, keepdims=True) + EPS) * g


def _sigmoid(x):
    return 1.0 / (1.0 + jnp.exp(-x))


def _silu(x):
    return x * _sigmoid(x)


def _softplus(x):
    neg_abs = pltpu.bitcast(pltpu.bitcast(x, jnp.uint32) | jnp.uint32(0x80000000), F32)
    return jnp.maximum(x, 0.0) + jnp.log(1.0 + jnp.exp(neg_abs))


def _softplus2(x):
    neg_abs = pltpu.bitcast(pltpu.bitcast(x, jnp.uint32) | jnp.uint32(0x80000000), F32)
    return jnp.maximum(x, 0.0) + jnp.log(1.0 + jnp.exp2(neg_abs)) * LOG2E


def _split_bf16(x):
    hi = x.astype(BF16)
    lo = (x - hi.astype(F32)).astype(BF16)
    return hi, lo


def _params(n_axes=1):
    return pltpu.CompilerParams(dimension_semantics=("arbitrary",) * n_axes,
                                vmem_limit_bytes=VMEM_LIMIT)


def _full(shape):
    return pl.BlockSpec(shape, lambda *_: (0,) * len(shape))


def _even_in_body(xs, rows, g_ref, w_ref, qg_ref, kg_ref,
                  q_ref, k_ref, v_ref, ga_ref, u_ref, gb_ref):
    width = SB_HEADS * SB_HEAD_DIM
    hs = [_rms(x, g_ref[...]).astype(BF16) for x in xs]
    parts = range(len(xs))
    seg = 512

    def proj(r, lo):
        return _dot(hs[r], w_ref[:, lo:lo + seg])

    for ref, gain_ref, off, scale in ((q_ref, qg_ref, 0, SB_HEAD_DIM ** -0.5 * LOG2E),
                                      (k_ref, kg_ref, width, 1.0)):
        gain = gain_ref[...] * scale
        for s in range(width // seg):
            for r in parts:
                p = proj(r, off + s * seg)
                for hd in range(seg // SB_HEAD_DIM):
                    ph = p[:, hd * SB_HEAD_DIM:(hd + 1) * SB_HEAD_DIM]
                    lo = s * seg + hd * SB_HEAD_DIM
                    ref[rows[r], lo:lo + SB_HEAD_DIM] = _rms(ph, gain).astype(BF16)
    for s in range(width // seg):
        for r in parts:
            v_ref[rows[r], s * seg:(s + 1) * seg] = proj(r, 2 * width + s * seg).astype(BF16)
    for s in range(width // seg):
        for r in parts:
            ga_ref[rows[r], s * seg:(s + 1) * seg] = _silu(proj(r, 3 * width + s * seg)).astype(BF16)
    for r in parts:
        gb_ref[rows[r], :] = _silu(proj(r, 4 * width + seg))
    for r in parts:
        u_ref[rows[r], :] = proj(r, 4 * width)


def _io_kernel(*refs, n_acts, in_kind):
    x_ref, refs = refs[0], refs[1:]
    tm = x_ref.shape[0]
    rows = [slice(r * tm // ROW_PARTS, (r + 1) * tm // ROW_PARTS) for r in range(ROW_PARTS)]
    xs = [x_ref[rw, :] for rw in rows]
    n_in = {None: 0, "even": 4, "odd": 5}[in_kind]
    if n_acts:
        act_refs, wout_ref, refs = refs[:n_acts], refs[n_acts], refs[n_acts + 1:]
        for r, rw in enumerate(rows):
            off = 0
            for a_ref in act_refs:
                kdim = a_ref.shape[1]
                xs[r] = xs[r] + _dot(a_ref[rw, :].astype(BF16), wout_ref[off:off + kdim, :])
                off += kdim
            refs[n_in][rw, :] = xs[r]
        in_refs, out_refs = refs[:n_in], refs[n_in + 1:]
    else:
        in_refs, out_refs = refs[:n_in], refs[n_in:]
    if in_kind == "even":
        _even_in_body(xs, rows, *in_refs, *out_refs)
    elif in_kind == "odd":
        _odd_in_body(xs, rows, *in_refs, *out_refs)


def _layer_io(x2, acts, w_out, in_kind, in_params):
    n, d = x2.shape
    row = lambda c: pl.BlockSpec((ROW_TILE, c), lambda i: (i, 0))
    once = lambda a: pl.BlockSpec(a.shape, lambda i: (0,) * a.ndim, pipeline_mode=pl.Buffered(1))
    sds = jax.ShapeDtypeStruct
    args, in_specs, out_specs, out_shape = [x2], [row(d)], [], []
    if acts:
        args += list(acts) + [w_out]
        in_specs += [row(a.shape[1]) for a in acts] + [once(w_out)]
        out_specs.append(row(d))
        out_shape.append(sds((n, d), F32))
    args += list(in_params)
    in_specs += [once(p) for p in in_params]
    if in_kind == "even":
        width = SB_HEADS * SB_HEAD_DIM
        s5w = (in_params[1].shape[1] - 4 * width) // 2
        outs = [(width, BF16)] * 4 + [(s5w, F32)] * 2
    elif in_kind == "odd":
        key, val = GLA_HEADS * GLA_DK, GLA_HEADS * GLA_DV
        outs = [(key, F32), (key, F32), (val, BF16), (val, BF16), (key, F32)]
    else:
        outs = []
    out_specs += [row(c) for c, _ in outs]
    out_shape += [sds((n, c), dt) for c, dt in outs]
    res = pl.pallas_call(
        functools.partial(_io_kernel, n_acts=len(acts), in_kind=in_kind),
        grid=(n // ROW_TILE,),
        in_specs=in_specs,
        out_specs=out_specs,
        out_shape=out_shape,
        compiler_params=_params(),
        name=("out_" if acts else "") + (in_kind + "_in_" if in_kind else "") + "proj",
    )(*args)
    return (res[0], res[1:]) if acts else (x2, res)


def _sb_attn_kernel(q_ref, k_ref, v_ref, gate_ref, uo_ref, o_ref, carry_ref, acc_ref):
    t = SB_BLOCK
    ng = SB_QGROUPS
    heads = q_ref.shape[2] // SB_HEAD_DIM
    base = pl.program_id(1) * ng
    uo = uo_ref[...]
    row = lax.broadcasted_iota(jnp.int32, (t, t), 0)
    col = lax.broadcasted_iota(jnp.int32, (t, t), 1)
    strict = col < row
    lanes = [slice(hd * SB_HEAD_DIM, (hd + 1) * SB_HEAD_DIM) for hd in range(heads)]
    qrows = [slice(g * t, (g + 1) * t) for g in range(ng)]

    def sweep(blocks):
        krows = {(g, n): pl.ds(pl.multiple_of(kb * t, t), t)
                 for g, blk in blocks.items() for n, (kb, _) in enumerate(blk)}
        items = [(g, n, hd) for (g, n) in krows for hd in range(heads)]
        zs = {(g, n, hd): _dot_nt(q_ref[0, qrows[g], lanes[hd]], k_ref[0, krows[g, n], lanes[hd]])
              for g, n, hd in items}
        r2s = {}
        for g, n, hd in items:
            sp = _softplus2(zs[g, n, hd])
            hi, lo = _split_bf16(jnp.where(strict, sp, 0.0) if blocks[g][n][1] else sp)
            r2s[g, n, hd] = _dot(jnp.concatenate([hi, lo], axis=1), uo)
        low = None
        for g, blk in blocks.items():
            fresh = blk[0][1]
            for hd in range(heads):
                carry = None if fresh else carry_ref[g, hd]
                pv = None
                for n, (_, diag) in enumerate(blk):
                    r2 = r2s[g, n, hd]
                    rest = r2[:, :t] if carry is None else r2[:, :t] + carry
                    carry = r2[:, t:] if carry is None else r2[:, t:] + carry
                    w = jnp.exp2(zs[g, n, hd] - rest)
                    if diag:
                        w = jnp.where(strict, w, 0.0)
                    d = _dot(w.astype(BF16), v_ref[0, krows[g, n], lanes[hd]])
                    pv = d if pv is None else pv + d
                carry_ref[g, hd] = carry
                low = carry if low is None else jnp.minimum(low, carry)
                acc_ref[g, hd] = pv if fresh else acc_ref[g, hd] + pv
        return jnp.min(low) < SB_DEAD_LOG2

    def first_step():
        sweep({g: [(g - n, n == 0) for n in range(g + 1)] for g in range(ng)})
        return jnp.bool_(False)

    def later_step():
        return sweep({g: [(base + g, True), (base + g - 1, False)] for g in range(ng)})

    def cond(state):
        j, live = state
        return jnp.logical_and(j <= base, live)

    def body(state):
        j, _ = state
        return j + 1, sweep({g: [(base + g - j, False)] for g in range(ng)})

    _, live = lax.while_loop(cond, body, (jnp.int32(2), lax.cond(base == 0, first_step, later_step)))

    for extra in range(1, ng):
        @pl.when(live)
        def _():
            sweep({g: [(g - extra, False)] for g in range(extra, ng)})

    for g in range(ng):
        for hd in range(heads):
            o_ref[0, qrows[g], lanes[hd]] = (
                acc_ref[g, hd] * gate_ref[0, qrows[g], lanes[hd]].astype(F32)).astype(BF16)


def _sb_attn(q, k, v, gate, uo):
    bsz, seq, width = q.shape
    t = SB_BLOCK
    tq = SB_QGROUPS * t
    heads = width // SB_HEAD_DIM
    qspec = pl.BlockSpec((1, tq, width), lambda b, i: (b, i, 0))
    kspec = pl.BlockSpec((1, seq, width), lambda b, i: (b, 0, 0))
    return pl.pallas_call(
        _sb_attn_kernel,
        grid=(bsz, seq // tq),
        in_specs=[qspec, kspec, kspec, qspec, _full(uo.shape)],
        out_specs=qspec,
        out_shape=jax.ShapeDtypeStruct((bsz, seq, width), BF16),
        scratch_shapes=[pltpu.VMEM((SB_QGROUPS, heads, t, t), F32),
                        pltpu.VMEM((SB_QGROUPS, heads, t, SB_HEAD_DIM), F32)],
        compiler_params=_params(2),
        name="sb_attention",
    )(q, k, v, gate, uo)


def _gelu_tanh(x):
    c = math.sqrt(2.0 / math.pi)
    return 0.5 * x * (1.0 + jnp.tanh(c * (x + 0.044715 * (x * x * x))))


def _s5_kernel(u_ref, gb_ref, bre_ref, bim_ref, cre_ref, cimn_ref, lre_ref, lim_ref,
               d_ref, wglu_ref, bglu_ref, o_ref,
               us_ref, ut_ref, bure_ref, buim_ref, hre_ref, him_ref,
               hsre_ref, hsim_ref, ys_ref, yn_ref):
    bsz, tt, width = u_ref.shape
    rows = bsz * tt
    nstate = bure_ref.shape[1]
    nsub = tt // S5_SUB
    nslab = width // LANES

    @pl.when(pl.program_id(0) == 0)
    def _():
        hsre_ref[...] = jnp.zeros_like(hsre_ref)
        hsim_ref[...] = jnp.zeros_like(hsim_ref)

    for c in range(nslab):
        for b in range(bsz):
            for sub in range(nsub):
                r0 = sub * bsz * S5_SUB + b * S5_SUB
                us_ref[c, r0:r0 + S5_SUB, :] = u_ref[b, sub * S5_SUB:(sub + 1) * S5_SUB,
                                                     c * LANES:(c + 1) * LANES]
    for sub in range(nsub):
        for t8 in range(S5_SUB):
            step = sub * S5_SUB + t8
            for c in range(nslab):
                ut_ref[step * bsz:(step + 1) * bsz, c * LANES:(c + 1) * LANES] = (
                    us_ref[c, pl.ds(sub * bsz * S5_SUB + t8, bsz, stride=S5_SUB), :])

    ut = ut_ref[...].astype(BF16)
    groups_per_tile = MXU_DIM // S5_STATE
    in_per_tile = groups_per_tile * S5_GROUP
    for j in range(nstate // MXU_DIM):
        kc = (j * in_per_tile) // MXU_DIM
        lhs = ut[:, kc * MXU_DIM:(kc + 1) * MXU_DIM]
        ksl = slice(kc * MXU_DIM, (kc + 1) * MXU_DIM)
        nsl = slice(j * MXU_DIM, (j + 1) * MXU_DIM)
        bure_ref[:, nsl] = _dot(lhs, bre_ref[ksl, nsl])
        buim_ref[:, nsl] = _dot(lhs, bim_ref[ksl, nsl])

    chunk = 512
    for c in range(nstate // chunk):
        sl = slice(c * chunk, (c + 1) * chunk)
        lr = jnp.broadcast_to(lre_ref[:, sl], (bsz, chunk))
        li = jnp.broadcast_to(lim_ref[:, sl], (bsz, chunk))
        hr = hsre_ref[:, sl]
        hi = hsim_ref[:, sl]
        for step in range(tt):
            rsl = slice(step * bsz, (step + 1) * bsz)
            hr, hi = (lr * hr - li * hi + bure_ref[rsl, sl],
                      lr * hi + li * hr + buim_ref[rsl, sl])
            hre_ref[rsl, sl] = hr.astype(BF16)
            him_ref[rsl, sl] = hi.astype(BF16)
        hsre_ref[:, sl] = hr
        hsim_ref[:, sl] = hi

    tiles_per_slab = (LANES // S5_GROUP) * S5_STATE // MXU_DIM
    for m in range(nslab):
        acc = None
        nsl = slice(m * LANES, (m + 1) * LANES)
        for k in range(m * tiles_per_slab, (m + 1) * tiles_per_slab):
            ksl = slice(k * MXU_DIM, (k + 1) * MXU_DIM)
            d = _dot(hre_ref[:, ksl], cre_ref[ksl, nsl]) + _dot(him_ref[:, ksl], cimn_ref[ksl, nsl])
            acc = d if acc is None else acc + d
        ys_ref[m] = acc
    for b in range(bsz):
        for c in range(nslab):
            yn_ref[b * tt:(b + 1) * tt, c * LANES:(c + 1) * LANES] = (
                ys_ref[c, pl.ds(b, tt, stride=bsz), :])

    u_nat = u_ref[...].reshape(rows, width)
    y = _gelu_tanh(yn_ref[...] + d_ref[...] * u_nat)
    glu = _dot(y.astype(BF16), wglu_ref[...]) + bglu_ref[...]
    out = y * _sigmoid(glu) * gb_ref[...].reshape(rows, width)
    o_ref[...] = out.reshape(bsz, tt, width)


def _s5(u, gb, bre, bim, cre, cimn, lre, lim, dskip, wglu, bglu):
    bsz, seq, width = u.shape
    nstate = bre.shape[1]
    tt = S5_TIME_TILE
    rows = bsz * tt
    blk = pl.BlockSpec((bsz, tt, width), lambda i: (0, i, 0))
    return pl.pallas_call(
        _s5_kernel,
        grid=(seq // tt,),
        in_specs=[blk, blk, _full(bre.shape), _full(bim.shape), _full(cre.shape),
                  _full(cimn.shape), _full(lre.shape), _full(lim.shape), _full(dskip.shape),
                  _full(wglu.shape), _full(bglu.shape)],
        out_specs=blk,
        out_shape=jax.ShapeDtypeStruct((bsz, seq, width), F32),
        scratch_shapes=[
            pltpu.VMEM((width // LANES, rows, LANES), F32),
            pltpu.VMEM((rows, width), F32),
            pltpu.VMEM((rows, nstate), F32),
            pltpu.VMEM((rows, nstate), F32),
            pltpu.VMEM((rows, nstate), BF16),
            pltpu.VMEM((rows, nstate), BF16),
            pltpu.VMEM((bsz, nstate), F32),
            pltpu.VMEM((bsz, nstate), F32),
            pltpu.VMEM((width // LANES, rows, LANES), F32),
            pltpu.VMEM((rows, width), F32),
        ],
        compiler_params=_params(),
        name="s5_mixer",
    )(u, gb, bre, bim, cre, cimn, lre, lim, dskip, wglu, bglu)


def _s5_weights(lam_re, lam_im, log_dt, b_re, b_im, c_re, c_im):
    g, n = lam_re.shape
    lam = lax.complex(lam_re.astype(F32), lam_im.astype(F32))
    dt = jnp.exp(log_dt.astype(F32))[:, None]
    lam_bar = jnp.exp(lam * dt)
    b_bar = ((lam_bar - 1.0) / lam)[..., None] * lax.complex(b_re.astype(F32), b_im.astype(F32))

    def block_diag(m):
        cols, rows = m.shape[1], m.shape[2]
        stacked = jnp.transpose(m.astype(BF16), (0, 2, 1)).reshape(g * rows, cols)
        r_grp = lax.broadcasted_iota(jnp.int32, (g * rows, g * cols), 0) // rows
        c_grp = lax.broadcasted_iota(jnp.int32, (g * rows, g * cols), 1) // cols
        return jnp.where(r_grp == c_grp, jnp.tile(stacked, (1, g)), jnp.zeros((), BF16))

    return (block_diag(b_bar.real), block_diag(b_bar.imag),
            block_diag(c_re), block_diag(-c_im),
            lam_bar.real.reshape(1, g * n), lam_bar.imag.reshape(1, g * n))


def _odd_in_body(xs, rows, g_ref, w_ref, wr_ref, wg_ref, bg_ref,
                 q_ref, k_ref, v_ref, gz_ref, la_ref):
    key = GLA_HEADS * GLA_DK
    val = GLA_HEADS * GLA_DV
    hs = [_rms(x, g_ref[...]).astype(BF16) for x in xs]
    parts = range(len(xs))
    seg = 512

    def proj(r, lo):
        return _dot(hs[r], w_ref[:, lo:lo + seg])

    codes = [_dot(hs[r], wr_ref[...]).astype(BF16) for r in parts]
    for r in parts:
        q_ref[rows[r], :] = proj(r, 0) * (GLA_DK ** -0.5)
    for r in parts:
        k_ref[rows[r], :] = proj(r, key)
    for r in parts:
        pre = _dot(codes[r], wg_ref[...]) + bg_ref[...]
        la_ref[rows[r], :] = -_softplus(-pre) * (1.0 / GLA_TAU)
    for s in range(val // seg):
        for r in parts:
            gz_ref[rows[r], s * seg:(s + 1) * seg] = _silu(proj(r, 2 * key + val + s * seg)).astype(BF16)
    for s in range(val // seg):
        for r in parts:
            v_ref[rows[r], s * seg:(s + 1) * seg] = proj(r, 2 * key + s * seg).astype(BF16)


def _gla_kernel(q_ref, k_ref, la_ref, v_ref, gz_ref, tri_ref, og_ref, o_ref,
                st_ref, qd_ref, sc_ref, ut_ref):
    c = GLA_CHUNK
    nchunk = q_ref.shape[1] // c
    items = [(i, hd) for i in range(nchunk) for hd in range(GLA_HEADS)]

    @pl.when(pl.program_id(1) == 0)
    def _():
        st_ref[...] = jnp.zeros_like(st_ref)

    tri2 = tri_ref[...]
    row = lax.broadcasted_iota(jnp.int32, (c, c), 0)
    col = lax.broadcasted_iota(jnp.int32, (c, c), 1)
    causal = col <= row
    og = og_ref[...]
    rows = lambda i: slice(i * c, (i + 1) * c)
    ksl = lambda hd: slice(hd * GLA_DK, (hd + 1) * GLA_DK)
    vsl = lambda hd: slice(hd * GLA_DV, (hd + 1) * GLA_DV)

    gs = []
    for i, hd in items:
        hi, lo = _split_bf16(la_ref[0, rows(i), ksl(hd)])
        gs.append(_dot(tri2, jnp.concatenate([hi, lo], axis=0)))
    k_invs, k_decs, e_lasts = [], [], []
    for n, (i, hd) in enumerate(items):
        g = gs[n]
        g_last = g[c - 1:c, :]
        k = k_ref[0, rows(i), ksl(hd)]
        qd_ref[i, hd] = (q_ref[0, rows(i), ksl(hd)] * jnp.exp(g)).astype(BF16)
        k_invs.append((k * jnp.exp(-g)).astype(BF16))
        k_decs.append((k * jnp.exp(g_last - g)).astype(BF16))
        e_lasts.append(jnp.exp(g_last))
    for n, (i, hd) in enumerate(items):
        scores = jnp.where(causal, _dot_nt(qd_ref[i, hd], k_invs[n]), 0.0)
        sc_ref[i, hd] = scores.astype(BF16)
    for n, (i, hd) in enumerate(items):
        ut_ref[i, hd] = lax.dot_general(v_ref[0, rows(i), vsl(hd)], k_decs[n],
                                        (((0,), (0,)), ((), ())), preferred_element_type=F32)

    for n, (i, hd) in enumerate(items):
        st = st_ref[hd]
        o = _dot(sc_ref[i, hd], v_ref[0, rows(i), vsl(hd)]) + _dot_nt(qd_ref[i, hd], st.astype(BF16))
        st_ref[hd] = st * e_lasts[n] + ut_ref[i, hd]
        o = _rms(o, og) * gz_ref[0, rows(i), vsl(hd)].astype(F32)
        o_ref[0, rows(i), vsl(hd)] = o.astype(BF16)


def _gla(q, k, la, v, gz, tri2, og):
    bsz, seq, key = q.shape
    val = v.shape[2]
    tt = GLA_TIME_TILE
    nchunk = tt // GLA_CHUNK
    kspec = pl.BlockSpec((1, tt, key), lambda b, i: (b, i, 0))
    vspec = pl.BlockSpec((1, tt, val), lambda b, i: (b, i, 0))
    return pl.pallas_call(
        _gla_kernel,
        grid=(bsz, seq // tt),
        in_specs=[kspec, kspec, kspec, vspec, vspec, _full(tri2.shape), _full(og.shape)],
        out_specs=vspec,
        out_shape=jax.ShapeDtypeStruct((bsz, seq, val), BF16),
        scratch_shapes=[
            pltpu.VMEM((GLA_HEADS, GLA_DV, GLA_DK), F32),
            pltpu.VMEM((nchunk, GLA_HEADS, GLA_CHUNK, GLA_DK), BF16),
            pltpu.VMEM((nchunk, GLA_HEADS, GLA_CHUNK, GLA_CHUNK), BF16),
            pltpu.VMEM((nchunk, GLA_HEADS, GLA_DV, GLA_DK), F32),
        ],
        compiler_params=_params(2),
        name="gla",
    )(q, k, la, v, gz, tri2, og)


def kernel(x, even_norm_g, even_w_in, sb_q_norm_g, sb_k_norm_g, s5_lambda_re, s5_lambda_im,
           s5_log_dt, s5_b_re, s5_b_im, s5_c_re, s5_c_im, s5_d, s5_w_glu, s5_b_glu, even_w_out,
           odd_norm_g, odd_w_in, gla_w_gate, gla_b_gate, gla_o_norm_g, odd_w_out):
    bsz, seq, d = x.shape
    n = bsz * seq
    depth = even_norm_g.shape[0] + odd_norm_g.shape[0]
    sb_width = SB_HEADS * SB_HEAD_DIM
    key = GLA_HEADS * GLA_DK
    val = GLA_HEADS * GLA_DV
    assert n % ROW_TILE == 0 and seq % (SB_QGROUPS * SB_BLOCK) == 0 and seq % GLA_TIME_TILE == 0
    assert seq % S5_TIME_TILE == 0 and bsz % 8 == 0

    t = SB_BLOCK
    j_idx = jnp.arange(t)[:, None]
    s_idx = jnp.arange(t)[None, :]
    uo = jnp.concatenate([(j_idx >= s_idx), jnp.ones((t, t), bool)], axis=1).astype(BF16)
    uo = jnp.concatenate([uo, uo], axis=0)
    c_idx = jnp.arange(GLA_CHUNK)
    tri = (c_idx[None, :] <= c_idx[:, None]).astype(BF16)
    tri = jnp.concatenate([tri, tri], axis=1)

    def in_side(layer):
        i = layer // 2
        if layer >= depth:
            return None, []
        if layer % 2 == 0:
            return "even", [even_norm_g[i].reshape(1, d).astype(F32), even_w_in_bf[i],
                            sb_q_norm_g[i].reshape(1, -1).astype(F32),
                            sb_k_norm_g[i].reshape(1, -1).astype(F32)]
        return "odd", [odd_norm_g[i].reshape(1, d).astype(F32), odd_w_main[i], odd_w_code[i],
                       gate_w[i], gla_b_gate[i].reshape(1, key).astype(F32)]

    main = 2 * key + 2 * val
    even_w_in_bf = even_w_in.astype(BF16)
    even_w_out_bf = even_w_out.astype(BF16)
    odd_w_main = odd_w_in.astype(BF16)
    odd_w_code = jnp.pad(odd_w_in[:, :, main:], ((0, 0), (0, 0), (0, LANES - GLA_RANK))).astype(BF16)
    gate_w = jnp.pad(gla_w_gate, ((0, 0), (0, LANES - GLA_RANK), (0, 0))).astype(BF16)
    odd_w_out_bf = odd_w_out.astype(BF16)
    glu_w = s5_w_glu.astype(BF16)
    s5_all = jax.vmap(_s5_weights)(s5_lambda_re, s5_lambda_im, s5_log_dt, s5_b_re, s5_b_im,
                                   s5_c_re, s5_c_im)

    x2 = x.reshape(n, d).astype(F32)
    x2, proj = _layer_io(x2, [], None, *in_side(0))
    for layer in range(depth):
        i = layer // 2
        if layer % 2 == 0:
            q, k, v, ga, u, gb = proj
            shp = (bsz, seq, sb_width)
            o_a = _sb_attn(q.reshape(shp), k.reshape(shp), v.reshape(shp), ga.reshape(shp), uo)
            s5w = u.shape[1]
            bre, bim, cre, cimn, lre, lim = [a[i] for a in s5_all]
            o_b = _s5(u.reshape(bsz, seq, s5w), gb.reshape(bsz, seq, s5w), bre, bim, cre, cimn,
                      lre, lim, s5_d[i].reshape(1, s5w).astype(F32), glu_w[i],
                      s5_b_glu[i].reshape(1, s5w).astype(F32))
            acts, w_out = [o_a.reshape(n, sb_width), o_b.reshape(n, s5w)], even_w_out_bf[i]
        else:
            q, k, v, gz, la = proj
            o = _gla(q.reshape(bsz, seq, key), k.reshape(bsz, seq, key), la.reshape(bsz, seq, key),
                     v.reshape(bsz, seq, val), gz.reshape(bsz, seq, val), tri,
                     gla_o_norm_g[i].reshape(1, GLA_DV).astype(F32))
            acts, w_out = [o.reshape(n, val)], odd_w_out_bf[i]
        x2, proj = _layer_io(x2, acts, w_out, *in_side(layer + 1))
    return x2.reshape(bsz, seq, d).astype(x.dtype)
```
